```python
import functools
import jax, jax.numpy as jnp
from jax import lax
import numpy as np

D_MODEL = 1024
BATCH = 2
SEQ = 8192
DEPTH = 4
DEC_BATCH = 32
DEC_SEQ = 1
PAST_LEN = 8192
PAGE_SIZE = 128

ATTN_WIDTH = D_MODEL // 2
CONV_WIDTH = D_MODEL - ATTN_WIDTH
N_HEADS = 8
HEAD_DIM = ATTN_WIDTH // N_HEADS
CONV_K = 31
N_META = 16
Q_BLOCK = 128
D_FF = -(-8 * D_MODEL // (3 * 256)) * 256
IN_WIDTH = 3 * ATTN_WIDTH + N_HEADS + 2 * CONV_WIDTH
EPS = 1e-6
NEG_INF = -1e30
FORGET_BIAS_LO = 2.0
FORGET_BIAS_HI = 6.0

kernel_name = 'fox_conformer_hymba_decoder_step'


def rmsnorm(x, g):
    xf = x.astype(jnp.float32)
    y = xf * lax.rsqrt(jnp.mean(xf * xf, axis=-1, keepdims=True) + EPS)
    return (y * g.astype(jnp.float32)).astype(x.dtype)


def layernorm(x, g, b):
    xf = x.astype(jnp.float32)
    mu = jnp.mean(xf, axis=-1, keepdims=True)
    xc = xf - mu
    y = xc * lax.rsqrt(jnp.mean(xc * xc, axis=-1, keepdims=True) + EPS)
    return (y * g.astype(jnp.float32) + b.astype(jnp.float32)).astype(x.dtype)


def fox_attend(q, k, v, f_q, f_k, pos_q, pos_k):
    s = jnp.einsum('bqhd,bkhd->bhqk', q, k, preferred_element_type=jnp.float32) * (HEAD_DIM ** -0.5)
    bias = jnp.swapaxes(f_q, 1, 2)[:, :, :, None] - jnp.swapaxes(f_k, 1, 2)[:, :, None, :]
    mask = pos_k[None, :] <= pos_q[:, None]
    s = jnp.where(mask, s + bias, NEG_INF)
    p = jax.nn.softmax(s, axis=-1)
    return jnp.einsum('bhqk,bkhd->bqhd', p.astype(v.dtype), v)


def prompt_attention(q, k, v, logf):
    B, T = q.shape[0], q.shape[1]
    f = jnp.cumsum(logf, axis=1)
    pos = jnp.arange(T)
    meta = fox_attend(q[:, :N_META], k[:, :N_META], v[:, :N_META],
                      f[:, :N_META], f[:, :N_META], pos[:N_META], pos[:N_META])
    n_blk = (T - N_META) // Q_BLOCK

    def one_block(b):
        start = N_META + b * Q_BLOCK
        qb = lax.dynamic_slice_in_dim(q, start, Q_BLOCK, axis=1)
        fb = lax.dynamic_slice_in_dim(f, start, Q_BLOCK, axis=1)
        return fox_attend(qb, k, v, fb, f, start + jnp.arange(Q_BLOCK), pos)

    blocks = lax.map(one_block, jnp.arange(n_blk))
    body = jnp.moveaxis(blocks, 0, 1).reshape(B, n_blk * Q_BLOCK, N_HEADS, HEAD_DIM)
    return jnp.concatenate([meta, body], axis=1)


def sample_attention(q, k, v, logf, k_past, v_past, logf_past):
    P, S = k_past.shape[1], q.shape[1]
    k_all = jnp.concatenate([k_past.astype(k.dtype), k], axis=1)
    v_all = jnp.concatenate([v_past.astype(v.dtype), v], axis=1)
    l_all = jnp.concatenate([logf_past.astype(jnp.float32), logf], axis=1)
    f = jnp.cumsum(l_all, axis=1)
    pos_k = jnp.arange(P + S)
    return fox_attend(q, k_all, v_all, f[:, P:], f, P + jnp.arange(S), pos_k)


def conv_module(glu, buf, w_dw, b_dw, g_ln, b_ln):
    ext = jnp.concatenate([buf.astype(glu.dtype), glu], axis=1)
    y = lax.conv_general_dilated(ext, w_dw[:, None, :].astype(ext.dtype), window_strides=(1,),
                                 padding='VALID', dimension_numbers=('NWC', 'WIO', 'NWC'),
                                 feature_group_count=CONV_WIDTH) + b_dw.astype(ext.dtype)
    y = jax.nn.silu(layernorm(y, g_ln, b_ln))
    return y, ext[:, -(CONV_K - 1):]


def trunk_layer(x, attend, conv_buf, w_in, b_forget, w_dw, b_dw, g_conv_ln, b_conv_ln, w_out,
                g_pre_mix, g_post_mix, g_pre_ffn, g_post_ffn, w_gate, w_up, w_down):
    B, T, _ = x.shape
    A, H, C = ATTN_WIDTH, N_HEADS, CONV_WIDTH
    h = rmsnorm(x, g_pre_mix)
    u = h @ w_in
    q, k, v, fg, ca, cb = jnp.split(u, [A, 2 * A, 3 * A, 3 * A + H, 3 * A + H + C], axis=-1)
    q = q.reshape(B, T, H, HEAD_DIM)
    k = k.reshape(B, T, H, HEAD_DIM)
    v = v.reshape(B, T, H, HEAD_DIM)
    logf = jax.nn.log_sigmoid(fg.astype(jnp.float32) + b_forget.astype(jnp.float32))
    attn = attend(q, k, v, logf)
    conv, new_buf = conv_module(ca * jax.nn.sigmoid(cb), conv_buf, w_dw, b_dw, g_conv_ln, b_conv_ln)
    mixed = jnp.concatenate([attn.reshape(B, T, A), conv], axis=-1) @ w_out
    x = x + rmsnorm(mixed, g_post_mix)
    h = rmsnorm(x, g_pre_ffn)
    x = x + rmsnorm((jax.nn.silu(h @ w_gate) * (h @ w_up)) @ w_down, g_post_ffn)
    return x, k, v, logf, new_buf


def setup_inputs(seed: int = 0) -> dict:
    key = jax.random.key(seed)
    ks = jax.random.split(key, 24)
    n_pages = PAST_LEN // PAGE_SIZE
    n_used = DEC_BATCH * n_pages
    n_pool = n_used + n_used // 4
    f32 = jnp.float32

    def nrm(k, shape, scale):
        return jax.random.normal(k, shape, f32) * scale

    def gain(k, shape):
        return 1.0 + 0.05 * jax.random.normal(k, shape, f32)

    x_prompt = nrm(ks[0], (BATCH, SEQ, D_MODEL), 1.0)
    x_sample = nrm(ks[1], (DEC_BATCH, DEC_SEQ, D_MODEL), 1.0)
    cache_k = nrm(ks[2], (DEPTH, n_pool, PAGE_SIZE, N_HEADS, HEAD_DIM), 1.0)
    cache_v = nrm(ks[3], (DEPTH, n_pool, PAGE_SIZE, N_HEADS, HEAD_DIM), 1.0)
    cache_logf = jax.nn.log_sigmoid(
        jax.random.uniform(ks[4], (DEPTH, n_pool, PAGE_SIZE, N_HEADS), f32, FORGET_BIAS_LO, FORGET_BIAS_HI)
        + jax.random.normal(ks[5], (DEPTH, n_pool, PAGE_SIZE, N_HEADS), f32))
    state_conv = nrm(ks[6], (DEPTH, DEC_BATCH, CONV_K - 1, CONV_WIDTH), 0.5)
    page_table = jax.random.permutation(ks[7], n_pool)[:n_used].reshape(DEC_BATCH, n_pages).astype(jnp.int32)
    meta_tokens = nrm(ks[8], (N_META, D_MODEL), 1.0)
    w_in = nrm(ks[9], (DEPTH, D_MODEL, IN_WIDTH), D_MODEL ** -0.5)
    b_forget = jax.random.uniform(ks[10], (DEPTH, N_HEADS), f32, FORGET_BIAS_LO, FORGET_BIAS_HI)
    w_dw = nrm(ks[11], (DEPTH, CONV_K, CONV_WIDTH), CONV_K ** -0.5)
    b_dw = nrm(ks[12], (DEPTH, CONV_WIDTH), 0.02)
    g_conv_ln = gain(ks[13], (DEPTH, CONV_WIDTH))
    b_conv_ln = nrm(ks[14], (DEPTH, CONV_WIDTH), 0.02)
    w_out = nrm(ks[15], (DEPTH, D_MODEL, D_MODEL), D_MODEL ** -0.5)
    g_pre_mix = gain(ks[16], (DEPTH, D_MODEL))
    g_post_mix = gain(ks[17], (DEPTH, D_MODEL))
    g_pre_ffn = gain(ks[18], (DEPTH, D_MODEL))
    g_post_ffn = gain(ks[19], (DEPTH, D_MODEL))
    w_gate = nrm(ks[20], (DEPTH, D_MODEL, D_FF), D_MODEL ** -0.5)
    w_up = nrm(ks[21], (DEPTH, D_MODEL, D_FF), D_MODEL ** -0.5)
    w_down = nrm(ks[22], (DEPTH, D_FF, D_MODEL), D_FF ** -0.5)
    return {'x_prompt': x_prompt, 'x_sample': x_sample, 'cache_k': cache_k, 'cache_v': cache_v,
            'cache_logf': cache_logf, 'state_conv': state_conv, 'page_table': page_table,
            'meta_tokens': meta_tokens, 'w_in': w_in, 'b_forget': b_forget, 'w_dw': w_dw,
            'b_dw': b_dw, 'g_conv_ln': g_conv_ln, 'b_conv_ln': b_conv_ln, 'w_out': w_out,
            'g_pre_mix': g_pre_mix, 'g_post_mix': g_post_mix, 'g_pre_ffn': g_pre_ffn,
            'g_post_ffn': g_post_ffn, 'w_gate': w_gate, 'w_up': w_up, 'w_down': w_down}


def reference(x_prompt, x_sample, cache_k, cache_v, cache_logf, state_conv, page_table,
              meta_tokens, w_in, b_forget, w_dw, b_dw, g_conv_ln, b_conv_ln, w_out,
              g_pre_mix, g_post_mix, g_pre_ffn, g_post_ffn, w_gate, w_up, w_down):
    B = x_prompt.shape[0]
    Bd = x_sample.shape[0]
    meta = jnp.broadcast_to(meta_tokens[None].astype(x_prompt.dtype), (B, N_META, D_MODEL))
    xp = jnp.concatenate([meta, x_prompt], axis=1)
    xs = x_sample
    kp_l, vp_l, lp_l, cp_l = [], [], [], []
    ks_l, vs_l, ls_l, cs_l = [], [], [], []
    for l in range(DEPTH):
        lw = (w_in[l], b_forget[l], w_dw[l], b_dw[l], g_conv_ln[l], b_conv_ln[l], w_out[l],
              g_pre_mix[l], g_post_mix[l], g_pre_ffn[l], g_post_ffn[l], w_gate[l], w_up[l], w_down[l])
        zero_buf = jnp.zeros((B, CONV_K - 1, CONV_WIDTH), xp.dtype)
        xp, k_new, v_new, l_new, c_new = trunk_layer(xp, prompt_attention, zero_buf, *lw)
        kp_l.append(k_new); vp_l.append(v_new); lp_l.append(l_new); cp_l.append(c_new)
        k_past = cache_k[l][page_table].reshape(Bd, -1, N_HEADS, HEAD_DIM)
        v_past = cache_v[l][page_table].reshape(Bd, -1, N_HEADS, HEAD_DIM)
        l_past = cache_logf[l][page_table].reshape(Bd, -1, N_HEADS)
        attend = functools.partial(sample_attention, k_past=k_past, v_past=v_past, logf_past=l_past)
        xs, k_new, v_new, l_new, c_new = trunk_layer(xs, attend, state_conv[l], *lw)
        ks_l.append(k_new); vs_l.append(v_new); ls_l.append(l_new); cs_l.append(c_new)
    y_prompt = xp[:, N_META:]
    y_sample = xs
    return (y_prompt, y_sample,
            jnp.stack(kp_l), jnp.stack(vp_l), jnp.stack(lp_l), jnp.stack(cp_l),
            jnp.stack(ks_l), jnp.stack(vs_l), jnp.stack(ls_l), jnp.stack(cs_l))
```

```python
import functools

import numpy as np
import jax
import jax.numpy as jnp
from jax import lax
from jax.experimental import pallas as pl
from jax.experimental.pallas import tpu as pltpu

N_HEADS = 8
HEAD_DIM = 64
ATTN_WIDTH = N_HEADS * HEAD_DIM
N_META = 16
EPS = 1e-6
NEG_INF = -1e30

LANES = 128
SUBLANES = 8
ROW_TILE = 640
ATTN_TILE = 640
CONV_CHUNK = 64
HIST_ROWS = 32
FF_CHUNK = 256
PAGES_PER_STEP = 4
VMEM_LIMIT = 56 * 1024 * 1024

F32 = jnp.float32
BF16 = jnp.bfloat16


def _resident(shape):
    zeros = (0,) * len(shape)
    return pl.BlockSpec(shape, lambda *_: zeros, pipeline_mode=pl.Buffered(1))


def _rmsnorm(x, g):
    return x * lax.rsqrt(jnp.mean(x * x, axis=-1, keepdims=True) + EPS) * g


def _log_sigmoid(x):
    return jnp.minimum(x, 0.0) - jnp.log1p(jnp.exp(-jnp.abs(x)))


def _sigmoid(x):
    return 1.0 / (1.0 + jnp.exp(-x))


def _split3(x):
    hi = x.astype(BF16)
    r1 = x - hi.astype(F32)
    mid = r1.astype(BF16)
    lo = (r1 - mid.astype(F32)).astype(BF16)
    return hi, mid, lo


def _conv_ln_silu(acc, g_ln, b_ln):
    mu = jnp.mean(acc, axis=-1, keepdims=True)
    xc = acc - mu
    y = xc * lax.rsqrt(jnp.mean(xc * xc, axis=-1, keepdims=True) + EPS) * g_ln + b_ln
    return y * _sigmoid(y)


def _prompt_in_kernel(x_ref, g_ref, w_ref, bf_ref, wdw_ref, bdw_ref, gln_ref, bln_ref, tri_ref,
                      q_ref, kb_ref, vb_ref, kf_ref, vf_ref, lf_ref, fcol_ref, frow_ref,
                      conv_ref, cst_ref, ext_ref, carry_ref, *, tm, n_tiles, t_real, conv_k):
    i = pl.program_id(1)
    a = ATTN_WIDTH

    @pl.when(i == 0)
    def _():
        ext_ref[0:HIST_ROWS, :] = jnp.zeros((HIST_ROWS, ext_ref.shape[1]), F32)
        carry_ref[...] = jnp.zeros_like(carry_ref)

    h = _rmsnorm(x_ref[...], g_ref[...]).astype(BF16)
    u = jnp.dot(h, w_ref[...], preferred_element_type=F32)
    q_ref[...] = u[:, 0:a].astype(BF16)
    k = u[:, a:2 * a]
    v = u[:, 2 * a:3 * a]
    kb_ref[...] = k.astype(BF16)
    vb_ref[...] = v.astype(BF16)
    kf_ref[...] = k
    vf_ref[...] = v

    logf = _log_sigmoid(u[:, 5 * a:5 * a + LANES] + bf_ref[...])
    lf_ref[...] = logf[:, 0:N_HEADS]
    tri = tri_ref[...]
    hi, mid, lo = _split3(logf)
    f = (jnp.dot(tri, hi, preferred_element_type=F32)
         + jnp.dot(tri, mid, preferred_element_type=F32)
         + jnp.dot(tri, lo, preferred_element_type=F32)) + carry_ref[...]
    carry_ref[...] = f[tm - 1:tm, :]
    fcol_ref[...] = f[:, 0:N_HEADS]
    frow_ref[...] = f.T[0:N_HEADS, :]

    ca = u[:, 3 * a:4 * a]
    cb = u[:, 4 * a:5 * a]
    ext_ref[HIST_ROWS:HIST_ROWS + tm, :] = ca * _sigmoid(cb)
    base = HIST_ROWS - (conv_k - 1)
    bdw = bdw_ref[...]
    gln = gln_ref[...]
    bln = bln_ref[...]
    for c in range(tm // CONV_CHUNK):
        r0 = c * CONV_CHUNK
        acc = jnp.broadcast_to(bdw, (CONV_CHUNK, bdw.shape[1]))
        for j in range(conv_k):
            acc = acc + wdw_ref[j:j + 1, :] * ext_ref[r0 + base + j:r0 + base + j + CONV_CHUNK, :]
        conv_ref[r0:r0 + CONV_CHUNK, :] = _conv_ln_silu(acc, gln, bln).astype(BF16)

    @pl.when(i == n_tiles - 1)
    def _():
        off = HIST_ROWS + (t_real - (conv_k - 1) - (n_tiles - 1) * tm)
        cst_ref[...] = ext_ref[off:off + conv_k - 1, :]

    ext_ref[0:HIST_ROWS, :] = ext_ref[tm:tm + HIST_ROWS, :]


def _prompt_in(x, g, w, bfg, wdw, bdw, gln, bln, tri, t_real):
    b, tp, d = x.shape
    tm = ROW_TILE
    n_tiles = tp // tm
    a = ATTN_WIDTH
    cw = wdw.shape[1]
    conv_k = wdw.shape[0]
    assert conv_k - 1 <= HIST_ROWS and t_real - (n_tiles - 1) * tm >= conv_k - 1
    row = lambda width: pl.BlockSpec((None, tm, width), lambda bi, i: (bi, i, 0))
    kernel = functools.partial(_prompt_in_kernel, tm=tm, n_tiles=n_tiles, t_real=t_real, conv_k=conv_k)
    out_shape = (
        jax.ShapeDtypeStruct((b, tp, a), BF16),
        jax.ShapeDtypeStruct((b, tp, a), BF16),
        jax.ShapeDtypeStruct((b, tp, a), BF16),
        jax.ShapeDtypeStruct((b, t_real, a), F32),
        jax.ShapeDtypeStruct((b, t_real, a), F32),
        jax.ShapeDtypeStruct((b, t_real, N_HEADS), F32),
        jax.ShapeDtypeStruct((b, tp, N_HEADS), F32),
        jax.ShapeDtypeStruct((b, N_HEADS, tp), F32),
        jax.ShapeDtypeStruct((b, tp, cw), BF16),
        jax.ShapeDtypeStruct((b, conv_k - 1, cw), F32),
    )
    out_specs = (
        row(a), row(a), row(a), row(a), row(a), row(N_HEADS), row(N_HEADS),
        pl.BlockSpec((None, N_HEADS, tm), lambda bi, i: (bi, 0, i)),
        row(cw),
        pl.BlockSpec((None, conv_k - 1, cw), lambda bi, i: (bi, 0, 0)),
    )
    in_specs = [
        row(d), _resident(g.shape), _resident(w.shape), _resident(bfg.shape), _resident(wdw.shape),
        _resident(bdw.shape), _resident(gln.shape), _resident(bln.shape), _resident(tri.shape),
    ]
    return pl.pallas_call(
        kernel,
        grid=(b, n_tiles),
        in_specs=in_specs,
        out_specs=out_specs,
        out_shape=out_shape,
        scratch_shapes=[pltpu.VMEM((tm + HIST_ROWS, cw), F32), pltpu.VMEM((1, LANES), F32)],
        compiler_params=pltpu.CompilerParams(
            dimension_semantics=("arbitrary", "arbitrary"), vmem_limit_bytes=VMEM_LIMIT),
        name="prompt_in",
    )(x, g, w, bfg, wdw, bdw, gln, bln, tri)


def _prompt_attn_kernel(qi_ref, kj_ref, q_ref, k_ref, v_ref, fcol_ref, frow_ref, o_ref,
                        qh_ref, fq_ref, m_ref, l_ref, acc_ref, *, tq, tk):
    hp = pl.program_id(1)
    s_idx = pl.program_id(2)
    i = qi_ref[s_idx]
    j = kj_ref[s_idx]
    lane = lax.broadcasted_iota(jnp.int32, (tq, LANES), 1)

    @pl.when(j == 0)
    def _():
        q = q_ref[...]
        zero = jnp.zeros_like(q)
        qh_ref[0] = jnp.where(lane < HEAD_DIM, q, zero)
        qh_ref[1] = jnp.where(lane >= HEAD_DIM, q, zero)
        fcol = fcol_ref[...]
        hl = lax.broadcasted_iota(jnp.int32, fcol.shape, 1)
        for hh in range(2):
            fq_ref[hh] = jnp.sum(jnp.where(hl == 2 * hp + hh, fcol, 0.0), axis=1, keepdims=True)
        m_ref[...] = jnp.full(m_ref.shape, NEG_INF, F32)
        l_ref[...] = jnp.zeros_like(l_ref)
        acc_ref[...] = jnp.zeros_like(acc_ref)

    def step(masked):
        k = k_ref[...]
        v = v_ref[...]
        for hh in range(2):
            s = lax.dot_general(qh_ref[hh], k, (((1,), (1,)), ((), ())), preferred_element_type=F32)
            fk = frow_ref[pl.ds(2 * hp + hh, 1), :]
            s = s + (fq_ref[hh] - fk)
            if masked:
                r = lax.broadcasted_iota(jnp.int32, (tq, tk), 0)
                c = lax.broadcasted_iota(jnp.int32, (tq, tk), 1)
                s = jnp.where(c <= r, s, NEG_INF)
            m_prev = m_ref[hh]
            m_new = jnp.maximum(m_prev, jnp.max(s, axis=1, keepdims=True))
            alpha = jnp.exp(m_prev - m_new)
            p = jnp.exp(s - m_new)
            l_ref[hh] = alpha * l_ref[hh] + jnp.sum(p, axis=1, keepdims=True)
            acc_ref[hh] = alpha * acc_ref[hh] + jnp.dot(p.astype(BF16), v, preferred_element_type=F32)
            m_ref[hh] = m_new

    @pl.when(j < i)
    def _():
        step(False)

    @pl.when(j == i)
    def _():
        step(True)
        o0 = acc_ref[0] / l_ref[0]
        o1 = acc_ref[1] / l_ref[1]
        o_ref[...] = jnp.where(lane < HEAD_DIM, o0, o1).astype(o_ref.dtype)


def _prompt_attn(q, k, v, fcol, frow):
    b, tp, a = q.shape
    tq = tk = ATTN_TILE
    nq = tp // tq
    qi = np.array([i for i in range(nq) for _ in range(i + 1)], np.int32)
    kj = np.array([j for i in range(nq) for j in range(i + 1)], np.int32)
    n_pairs = a // LANES
    grid_spec = pltpu.PrefetchScalarGridSpec(
        num_scalar_prefetch=2,
        grid=(b, n_pairs, len(qi)),
        in_specs=[
            pl.BlockSpec((None, tq, LANES), lambda bi, hp, s, qi, kj: (bi, qi[s], hp)),
            pl.BlockSpec((None, tk, LANES), lambda bi, hp, s, qi, kj: (bi, kj[s], hp)),
            pl.BlockSpec((None, tk, LANES), lambda bi, hp, s, qi, kj: (bi, kj[s], hp)),
            pl.BlockSpec((None, tq, N_HEADS), lambda bi, hp, s, qi, kj: (bi, qi[s], 0)),
            pl.BlockSpec((None, N_HEADS, tk), lambda bi, hp, s, qi, kj: (bi, 0, kj[s])),
        ],
        out_specs=pl.BlockSpec((None, tq, LANES), lambda bi, hp, s, qi, kj: (bi, qi[s], hp)),
        scratch_shapes=[
            pltpu.VMEM((2, tq, LANES), BF16),
            pltpu.VMEM((2, tq, 1), F32),
            pltpu.VMEM((2, tq, 1), F32),
            pltpu.VMEM((2, tq, 1), F32),
            pltpu.VMEM((2, tq, LANES), F32),
        ],
    )
    return pl.pallas_call(
        functools.partial(_prompt_attn_kernel, tq=tq, tk=tk),
        grid_spec=grid_spec,
        out_shape=jax.ShapeDtypeStruct((b, tp, a), BF16),
        compiler_params=pltpu.CompilerParams(
            dimension_semantics=("arbitrary", "arbitrary", "arbitrary"), vmem_limit_bytes=VMEM_LIMIT),
        name="prompt_attn",
    )(jnp.asarray(qi), jnp.asarray(kj), q, k, v, fcol, frow)


def _out_ffn_kernel(x_ref, attn_ref, conv_ref, woa_ref, woc_ref, gpm_ref, gpf_ref, gqf_ref,
                    wg_ref, wu_ref, wd_ref, o_ref, *, d_ff):
    mixed = (jnp.dot(attn_ref[...], woa_ref[...], preferred_element_type=F32)
             + jnp.dot(conv_ref[...], woc_ref[...], preferred_element_type=F32))
    x1 = x_ref[...] + _rmsnorm(mixed, gpm_ref[...])
    h = _rmsnorm(x1, gpf_ref[...]).astype(BF16)
    ff = jnp.zeros(x1.shape, F32)
    for c in range(d_ff // FF_CHUNK):
        c0 = c * FF_CHUNK
        gate = jnp.dot(h, wg_ref[:, c0:c0 + FF_CHUNK], preferred_element_type=F32)
        up = jnp.dot(h, wu_ref[:, c0:c0 + FF_CHUNK], preferred_element_type=F32)
        act = (gate * _sigmoid(gate) * up).astype(BF16)
        ff = ff + jnp.dot(act, wd_ref[c0:c0 + FF_CHUNK, :], preferred_element_type=F32)
    o_ref[...] = x1 + _rmsnorm(ff, gqf_ref[...])


def _out_ffn(x, attn, conv, woa, woc, gpm, gpf, gqf, wg, wu, wd, tm):
    b, tp, d = x.shape
    d_ff = wg.shape[1]
    row = lambda width: pl.BlockSpec((None, tm, width), lambda bi, i: (bi, i, 0))
    in_specs = [row(d), row(attn.shape[2]), row(conv.shape[2])] + [
        _resident(w.shape) for w in (woa, woc, gpm, gpf, gqf, wg, wu, wd)]
    return pl.pallas_call(
        functools.partial(_out_ffn_kernel, d_ff=d_ff),
        grid=(b, tp // tm),
        in_specs=in_specs,
        out_specs=row(d),
        out_shape=jax.ShapeDtypeStruct((b, tp, d), F32),
        compiler_params=pltpu.CompilerParams(
            dimension_semantics=("arbitrary", "arbitrary"), vmem_limit_bytes=VMEM_LIMIT),
        name="out_ffn",
    )(x, attn, conv, woa, woc, gpm, gpf, gqf, wg, wu, wd)


def _sample_in_kernel(x_ref, g_ref, w_ref, bf_ref, st_ref, wdw_ref, bdw_ref, gln_ref, bln_ref,
                      q_ref, k_ref, v_ref, lf_ref, glu_ref, conv_ref, *, conv_k):
    a = ATTN_WIDTH
    h = _rmsnorm(x_ref[...], g_ref[...]).astype(BF16)
    u = jnp.dot(h, w_ref[...], preferred_element_type=F32)
    q_ref[...] = u[:, 0:a]
    k_ref[...] = u[:, a:2 * a]
    v_ref[...] = u[:, 2 * a:3 * a]
    lf_ref[...] = _log_sigmoid(u[:, 5 * a:5 * a + LANES] + bf_ref[...])[:, 0:N_HEADS]
    glu = u[:, 3 * a:4 * a] * _sigmoid(u[:, 4 * a:5 * a])
    glu_ref[...] = glu
    wdw = wdw_ref[...]
    hist = jnp.sum(st_ref[...] * wdw[0:conv_k - 1, :][None], axis=1)
    acc = hist + wdw[conv_k - 1:conv_k, :] * glu + bdw_ref[...]
    conv_ref[...] = _conv_ln_silu(acc, gln_ref[...], bln_ref[...]).astype(BF16)


def _sample_in(x, g, w, bfg, state, layer, wdw, bdw, gln, bln):
    n, d = x.shape
    a = ATTN_WIDTH
    cw = wdw.shape[1]
    conv_k = wdw.shape[0]
    full = lambda arr: pl.BlockSpec(arr.shape, lambda i: (0,) * arr.ndim)
    out = lambda width, dt: (jax.ShapeDtypeStruct((n, width), dt), pl.BlockSpec((n, width), lambda i: (0, 0)))
    outs = [out(a, F32), out(a, F32), out(a, F32), out(N_HEADS, F32), out(cw, F32), out(cw, BF16)]
    st_spec = pl.BlockSpec((None,) + state.shape[1:], lambda i: (layer, 0, 0, 0))
    return pl.pallas_call(
        functools.partial(_sample_in_kernel, conv_k=conv_k),
        grid=(1,),
        in_specs=[full(x), full(g), full(w), full(bfg), st_spec, full(wdw), full(bdw), full(gln), full(bln)],
        out_specs=tuple(o[1] for o in outs),
        out_shape=tuple(o[0] for o in outs),
        compiler_params=pltpu.CompilerParams(
            dimension_semantics=("arbitrary",), vmem_limit_bytes=VMEM_LIMIT),
        name="sample_in",
    )(x, g, w, bfg, state, wdw, bdw, gln, bln)


def _sample_attn_kernel(pt_ref, q_ref, kn_ref, vn_ref, lfn_ref, us_ref, *rest, pps, page):
    k_refs = rest[0:pps]
    v_refs = rest[pps:2 * pps]
    x_refs = rest[2 * pps:3 * pps]
    o_ref = rest[3 * pps]
    m_ref, l_ref, acc_ref, carry_ref = rest[3 * pps + 1:]
    g = pl.program_id(1)
    width = page * N_HEADS

    @pl.when(g == 0)
    def _():
        m_ref[...] = jnp.full(m_ref.shape, NEG_INF, F32)
        l_ref[...] = jnp.zeros_like(l_ref)
        acc_ref[...] = jnp.zeros_like(acc_ref)
        carry_ref[...] = jnp.zeros_like(carry_ref)

    q = q_ref[...]
    sub = lax.broadcasted_iota(jnp.int32, (N_HEADS, width), 0)
    col = lax.broadcasted_iota(jnp.int32, (N_HEADS, width), 1)
    diag = (col % N_HEADS) == sub
    us = us_ref[...]
    lfn = lfn_ref[...]

    for idx in range(pps):
        hi, mid, lo = _split3(x_refs[idx][...])
        yr = (jnp.dot(hi, us, preferred_element_type=F32)
              + jnp.dot(mid, us, preferred_element_type=F32)
              + jnp.dot(lo, us, preferred_element_type=F32))
        within = yr[:, 0:LANES]
        row_tot = yr[:, LANES:2 * LANES]
        run = carry_ref[...]
        rows = [None] * SUBLANES
        for r in range(SUBLANES - 1, -1, -1):
            rows[r] = jnp.broadcast_to(within[r:r + 1, :] + run, (N_HEADS, LANES))
            run = run + row_tot[r:r + 1, :]
        carry_ref[...] = run
        bias = jnp.concatenate(rows, axis=1)

        k2 = k_refs[idx][...].reshape(width, HEAD_DIM)
        v2 = v_refs[idx][...].reshape(width, HEAD_DIM)
        s = lax.dot_general(q, k2, (((1,), (1,)), ((), ())), preferred_element_type=F32)
        s = jnp.where(diag, s + (bias + lfn), NEG_INF)
        m_prev = m_ref[...]
        m_new = jnp.maximum(m_prev, jnp.max(s, axis=1, keepdims=True))
        alpha = jnp.exp(m_prev - m_new)
        p = jnp.exp(s - m_new)
        l_ref[...] = alpha * l_ref[...] + jnp.sum(p, axis=1, keepdims=True)
        acc_ref[...] = alpha * acc_ref[...] + jnp.dot(p, v2, preferred_element_type=F32)
        m_ref[...] = m_new

    @pl.when(g == pl.num_programs(1) - 1)
    def _():
        s_new = jnp.sum(q * kn_ref[...], axis=1, keepdims=True)
        m_prev = m_ref[...]
        m_fin = jnp.maximum(m_prev, s_new)
        alpha = jnp.exp(m_prev - m_fin)
        p_new = jnp.exp(s_new - m_fin)
        denom = alpha * l_ref[...] + p_new
        o_ref[...] = (alpha * acc_ref[...] + p_new * vn_ref[...]) / denom


def _sample_attn(page_table, q, kn, vn, lfn, us, cache_k, cache_v, cache_x, layer):
    n, n_pages = page_table.shape
    page = cache_k.shape[2]
    pps = PAGES_PER_STEP
    steps = n_pages // pps
    per_seq = lambda arr: pl.BlockSpec((None,) + arr.shape[1:], lambda bi, g, pt: (bi,) + (0,) * (arr.ndim - 1))

    def paged(arr, idx):
        zeros = (0,) * (arr.ndim - 2)
        return pl.BlockSpec(
            (None, None) + arr.shape[2:],
            lambda bi, g, pt: (layer, pt[bi, n_pages - 1 - (g * pps + idx)]) + zeros)

    in_specs = [per_seq(q), per_seq(kn), per_seq(vn), per_seq(lfn),
                pl.BlockSpec(us.shape, lambda bi, g, pt: (0, 0))]
    in_specs += [paged(cache_k, idx) for idx in range(pps)]
    in_specs += [paged(cache_v, idx) for idx in range(pps)]
    in_specs += [paged(cache_x, idx) for idx in range(pps)]
    grid_spec = pltpu.PrefetchScalarGridSpec(
        num_scalar_prefetch=1,
        grid=(n, steps),
        in_specs=in_specs,
        out_specs=per_seq(q),
        scratch_shapes=[
            pltpu.VMEM((N_HEADS, 1), F32), pltpu.VMEM((N_HEADS, 1), F32),
            pltpu.VMEM((N_HEADS, HEAD_DIM), F32), pltpu.VMEM((1, LANES), F32)],
    )
    args = [page_table, q, kn, vn, lfn, us] + [cache_k] * pps + [cache_v] * pps + [cache_x] * pps
    return pl.pallas_call(
        functools.partial(_sample_attn_kernel, pps=pps, page=page),
        grid_spec=grid_spec,
        out_shape=jax.ShapeDtypeStruct(q.shape, F32),
        compiler_params=pltpu.CompilerParams(
            dimension_semantics=("arbitrary", "arbitrary"), vmem_limit_bytes=VMEM_LIMIT),
        name="sample_attn",
    )(*args)


def _suffix_matrices():
    src = np.arange(LANES)[:, None]
    dst = np.arange(LANES)[None, :]
    same = (src % N_HEADS) == (dst % N_HEADS)
    after = same & (src // N_HEADS > dst // N_HEADS)
    return jnp.asarray(np.concatenate([after, same], axis=1).astype(np.float32), BF16)


def kernel(x_prompt, x_sample, cache_k, cache_v, cache_logf, state_conv, page_table, meta_tokens,
           w_in, b_forget, w_dw, b_dw, g_conv_ln, b_conv_ln, w_out, g_pre_mix, g_post_mix,
           g_pre_ffn, g_post_ffn, w_gate, w_up, w_down):
    depth = w_in.shape[0]
    b, seq, d = x_prompt.shape
    n = x_sample.shape[0]
    a = ATTN_WIDTH
    cw = w_dw.shape[2]
    t_real = N_META + seq
    tp = -(-t_real // ROW_TILE) * ROW_TILE

    scale = HEAD_DIM ** -0.5
    fg0 = 3 * a
    ca0 = fg0 + N_HEADS
    w_in_r = jnp.concatenate([
        w_in[:, :, 0:a] * scale, w_in[:, :, a:3 * a], w_in[:, :, ca0:ca0 + 2 * cw],
        jnp.pad(w_in[:, :, fg0:ca0], ((0, 0), (0, 0), (0, LANES - N_HEADS)))], axis=2).astype(BF16)
    bfg = jnp.pad(b_forget, ((0, 0), (0, LANES - N_HEADS)))[:, None, :]
    w_out_b = w_out.astype(BF16)
    w_gate_b = w_gate.astype(BF16)
    w_up_b = w_up.astype(BF16)
    w_down_b = w_down.astype(BF16)
    vec = lambda p: p[:, None, :]
    tri = jnp.asarray(np.tril(np.ones((ROW_TILE, ROW_TILE), np.float32)), BF16)
    us = _suffix_matrices()
    cache_x = cache_logf.reshape(cache_logf.shape[0], cache_logf.shape[1], SUBLANES, -1)
    cache_k4 = cache_k
    cache_v4 = cache_v

    meta = jnp.broadcast_to(meta_tokens[None].astype(x_prompt.dtype), (b, N_META, d))
    xp = jnp.concatenate([meta, x_prompt, jnp.zeros((b, tp - t_real, d), x_prompt.dtype)], axis=1)
    xs = x_sample.reshape(n, d)

    kp, vp, lp, cp, ks, vs, ls, cs = ([] for _ in range(8))
    for l in range(depth):
        lw_in = (vec(g_pre_mix)[l], w_in_r[l], bfg[l])
        lw_conv = (w_dw[l], vec(b_dw)[l], vec(g_conv_ln)[l], vec(b_conv_ln)[l])
        lw_out = (w_out_b[l, 0:a], w_out_b[l, a:], vec(g_post_mix)[l], vec(g_pre_ffn)[l],
                  vec(g_post_ffn)[l], w_gate_b[l], w_up_b[l], w_down_b[l])

        q, kb, vb, kf, vf, lf, fcol, frow, conv, cst = _prompt_in(xp, *lw_in, *lw_conv, tri, t_real)
        attn = _prompt_attn(q, kb, vb, fcol, frow)
        xp = _out_ffn(xp, attn, conv, *lw_out, tm=ROW_TILE)
        kp.append(kf); vp.append(vf); lp.append(lf); cp.append(cst)

        qs, k_new, v_new, lf_new, glu, conv_s = _sample_in(xs, *lw_in, state_conv, l, *lw_conv)
        heads = lambda t: t.reshape(n, N_HEADS, HEAD_DIM)
        o = _sample_attn(page_table, heads(qs), heads(k_new), heads(v_new), lf_new[:, :, None], us,
                         cache_k4, cache_v4, cache_x, l)
        xs = _out_ffn(xs[None], o.reshape(1, n, a).astype(BF16), conv_s[None], *lw_out, tm=n)[0]
        ks.append(k_new); vs.append(v_new); ls.append(lf_new)
        cs.append(jnp.concatenate([state_conv[l, :, 1:, :], glu[:, None, :]], axis=1))

    hd = (N_HEADS, HEAD_DIM)
    return (xp[:, N_META:t_real], xs[:, None, :],
            jnp.stack(kp).reshape(depth, b, t_real, *hd), jnp.stack(vp).reshape(depth, b, t_real, *hd),
            jnp.stack(lp), jnp.stack(cp),
            jnp.stack(ks).reshape(depth, n, 1, *hd), jnp.stack(vs).reshape(depth, n, 1, *hd),
            jnp.stack(ls)[:, :, None, :], jnp.stack(cs))
```

```python
import functools

import numpy as np
import jax
import jax.numpy as jnp
from jax import lax
from jax.experimental import pallas as pl
from jax.experimental.pallas import tpu as pltpu

N_HEADS = 8
HEAD_DIM = 64
ATTN_WIDTH = N_HEADS * HEAD_DIM
N_META = 16
EPS = 1e-6
NEG_INF = -1e30

LANES = 128
SUBLANES = 8
HEAD_TILES = N_HEADS * LANES
ATTN_TILE = 768
ROW_TILE = 384
CONV_CHUNK = 64
HIST_ROWS = 32
FF_CHUNK = 256
PAGES_PER_STEP = 8
VMEM_LIMIT = 56 * 1024 * 1024

F32 = jnp.float32
BF16 = jnp.bfloat16


def _aux_base(head):
    return HEAD_DIM if head % 2 == 0 else 0


def _resident(shape):
    zeros = (0,) * len(shape)
    return pl.BlockSpec(shape, lambda *_: zeros, pipeline_mode=pl.Buffered(1))


def _rmsnorm(x, g):
    return x * lax.rsqrt(jnp.mean(x * x, axis=-1, keepdims=True) + EPS) * g


def _log_sigmoid(x):
    return jnp.minimum(x, 0.0) - jnp.log1p(jnp.exp(-jnp.abs(x)))


def _sigmoid(x):
    return 1.0 / (1.0 + jnp.exp(-x))


def _split3(x):
    hi = x.astype(BF16)
    r1 = x - hi.astype(F32)
    mid = r1.astype(BF16)
    lo = (r1 - mid.astype(F32)).astype(BF16)
    return hi, mid, lo


def _conv_ln_silu(acc, g_ln, b_ln):
    mu = jnp.mean(acc, axis=-1, keepdims=True)
    xc = acc - mu
    y = xc * lax.rsqrt(jnp.mean(xc * xc, axis=-1, keepdims=True) + EPS) * g_ln + b_ln
    return y * _sigmoid(y)


def _compact_heads(tiles):
    pairs = [tiles[:, (2 * p) * LANES:(2 * p + 1) * LANES] + tiles[:, (2 * p + 1) * LANES:(2 * p + 2) * LANES]
             for p in range(N_HEADS // 2)]
    return jnp.concatenate(pairs, axis=1)


def _prompt_in_kernel(x_ref, g_ref, w_ref, bf_ref, wdw_ref, bdw_ref, gln_ref, bln_ref, tri_ref,
                      eq_ref, ek_ref, cq_ref, ck_ref, cv_ref,
                      q_ref, k_ref, v_ref, kf_ref, vf_ref, lf_ref, conv_ref, cst_ref,
                      ext_ref, carry_ref, *, tm, n_tiles, t_real, conv_k):
    i = pl.program_id(1)
    ht = HEAD_TILES
    cw = conv_ref.shape[1]

    @pl.when(i == 0)
    def _():
        ext_ref[0:HIST_ROWS, :] = jnp.zeros((HIST_ROWS, cw), F32)
        carry_ref[...] = jnp.zeros_like(carry_ref)

    h = _rmsnorm(x_ref[...], g_ref[...]).astype(BF16)
    u = jnp.dot(h, w_ref[...], preferred_element_type=F32)
    uq = u[:, 0:ht]
    uk = u[:, ht:2 * ht]
    uv = u[:, 2 * ht:3 * ht]
    kf_ref[...] = _compact_heads(uk)
    vf_ref[...] = _compact_heads(uv)

    fg0 = 3 * ht + 2 * cw
    logf = _log_sigmoid(u[:, fg0:fg0 + LANES] + bf_ref[...])
    lf_ref[...] = logf[:, 0:N_HEADS]
    tri = tri_ref[...]
    hi, mid, lo = _split3(logf)
    f = (jnp.dot(tri, hi, preferred_element_type=F32)
         + jnp.dot(tri, mid, preferred_element_type=F32)
         + jnp.dot(tri, lo, preferred_element_type=F32)) + carry_ref[...]
    carry_ref[...] = f[tm - 1:tm, :]
    fparts = jnp.concatenate(_split3(f), axis=1)
    qt = (uq + jnp.dot(fparts, eq_ref[...], preferred_element_type=F32) + cq_ref[...]).astype(BF16)
    kt = (uk + jnp.dot(fparts, ek_ref[...], preferred_element_type=F32) + ck_ref[...]).astype(BF16)
    vt = (uv + cv_ref[...]).astype(BF16)
    for hd in range(N_HEADS):
        q_ref[hd] = qt[:, hd * LANES:(hd + 1) * LANES]
        k_ref[hd] = kt[:, hd * LANES:(hd + 1) * LANES]
        v_ref[hd] = vt[:, hd * LANES:(hd + 1) * LANES]

    ca = u[:, 3 * ht:3 * ht + cw]
    cb = u[:, 3 * ht + cw:3 * ht + 2 * cw]
    ext_ref[HIST_ROWS:HIST_ROWS + tm, :] = ca * _sigmoid(cb)
    base = HIST_ROWS - (conv_k - 1)
    bdw = bdw_ref[...]
    gln = gln_ref[...]
    bln = bln_ref[...]
    for c in range(tm // CONV_CHUNK):
        r0 = c * CONV_CHUNK
        acc = jnp.broadcast_to(bdw, (CONV_CHUNK, cw))
        for j in range(conv_k):
            acc = acc + wdw_ref[j:j + 1, :] * ext_ref[r0 + base + j:r0 + base + j + CONV_CHUNK, :]
        conv_ref[r0:r0 + CONV_CHUNK, :] = _conv_ln_silu(acc, gln, bln).astype(BF16)

    last = (t_real - 1) // tm

    @pl.when(i == last)
    def _():
        off = HIST_ROWS + (t_real - (conv_k - 1) - last * tm)
        cst_ref[...] = ext_ref[off:off + conv_k - 1, :]

    ext_ref[0:HIST_ROWS, :] = ext_ref[tm:tm + HIST_ROWS, :]


def _prompt_in(x, g, w, bfg, wdw, bdw, gln, bln, tri, eq, ek, cq, ck, cv, t_real):
    b, tp, d = x.shape
    tm = ROW_TILE
    n_tiles = tp // tm
    a = ATTN_WIDTH
    cw = wdw.shape[1]
    conv_k = wdw.shape[0]
    assert conv_k - 1 <= HIST_ROWS and t_real - ((t_real - 1) // tm) * tm >= conv_k - 1
    row = lambda width: pl.BlockSpec((None, tm, width), lambda bi, i: (bi, i, 0))
    tiles = pl.BlockSpec((None, N_HEADS, tm, LANES), lambda bi, i: (bi, 0, i, 0))
    kernel = functools.partial(_prompt_in_kernel, tm=tm, n_tiles=n_tiles, t_real=t_real, conv_k=conv_k)
    out_shape = (
        jax.ShapeDtypeStruct((b, N_HEADS, tp, LANES), BF16),
        jax.ShapeDtypeStruct((b, N_HEADS, tp, LANES), BF16),
        jax.ShapeDtypeStruct((b, N_HEADS, tp, LANES), BF16),
        jax.ShapeDtypeStruct((b, t_real, a), F32),
        jax.ShapeDtypeStruct((b, t_real, a), F32),
        jax.ShapeDtypeStruct((b, t_real, N_HEADS), F32),
        jax.ShapeDtypeStruct((b, tp, cw), BF16),
        jax.ShapeDtypeStruct((b, conv_k - 1, cw), F32),
    )
    out_specs = (tiles, tiles, tiles, row(a), row(a), row(N_HEADS), row(cw),
                 pl.BlockSpec((None, conv_k - 1, cw), lambda bi, i: (bi, 0, 0)))
    consts = (g, w, bfg, wdw, bdw, gln, bln, tri, eq, ek, cq, ck, cv)
    return pl.pallas_call(
        kernel,
        grid=(b, n_tiles),
        in_specs=[row(d)] + [_resident(c.shape) for c in consts],
        out_specs=out_specs,
        out_shape=out_shape,
        scratch_shapes=[pltpu.VMEM((tm + HIST_ROWS, cw), F32), pltpu.VMEM((1, LANES), F32)],
        compiler_params=pltpu.CompilerParams(
            dimension_semantics=("arbitrary", "arbitrary"), vmem_limit_bytes=VMEM_LIMIT),
        name="prompt_in",
    )(x, *consts)


def _prompt_attn_kernel(q_ref, k_ref, v_ref, o_ref, m_ref, acc_ref, *, tq):
    i = pl.program_id(2)
    m_ref[...] = jnp.full(m_ref.shape, NEG_INF, F32)
    acc_ref[...] = jnp.zeros_like(acc_ref)

    def step(j, masked):
        start = pl.multiple_of(j * tq, tq)
        for hh in range(2):
            kc = k_ref[hh, pl.ds(start, tq), :]
            vc = v_ref[hh, pl.ds(start, tq), :]
            s = lax.dot_general(q_ref[hh], kc, (((1,), (1,)), ((), ())), preferred_element_type=F32)
            if masked:
                r = lax.broadcasted_iota(jnp.int32, (tq, tq), 0)
                c = lax.broadcasted_iota(jnp.int32, (tq, tq), 1)
                s = jnp.where(c <= r, s, NEG_INF)
            m_prev = m_ref[hh]
            m_new = jnp.maximum(m_prev, jnp.max(s, axis=1, keepdims=True))
            alpha = jnp.exp(m_prev - m_new)
            p = jnp.exp(s - jnp.tile(m_new, (1, tq // LANES)))
            acc_ref[hh] = alpha * acc_ref[hh] + jnp.dot(p.astype(BF16), vc, preferred_element_type=F32)
            m_ref[hh] = m_new

    def body(j, carry):
        step(j, False)
        return carry

    lax.fori_loop(0, i, body, 0)
    step(i, True)
    for hh in range(2):
        acc = acc_ref[hh]
        ab = _aux_base(hh)
        o_ref[hh] = (acc / acc[:, ab:ab + 1]).astype(o_ref.dtype)


def _prompt_attn(q, k, v):
    b, nh, tp, _ = q.shape
    tq = ATTN_TILE
    qspec = pl.BlockSpec((None, 2, tq, LANES), lambda bi, hp, i: (bi, hp, i, 0))
    kspec = pl.BlockSpec((None, 2, tp, LANES), lambda bi, hp, i: (bi, hp, 0, 0))
    return pl.pallas_call(
        functools.partial(_prompt_attn_kernel, tq=tq),
        grid=(b, nh // 2, tp // tq),
        in_specs=[qspec, kspec, kspec],
        out_specs=qspec,
        out_shape=jax.ShapeDtypeStruct(q.shape, BF16),
        scratch_shapes=[pltpu.VMEM((2, tq, LANES), F32), pltpu.VMEM((2, tq, LANES), F32)],
        compiler_params=pltpu.CompilerParams(
            dimension_semantics=("arbitrary", "arbitrary", "arbitrary"), vmem_limit_bytes=VMEM_LIMIT),
        name="prompt_attn",
    )(q, k, v)


def _out_ffn_kernel(x_ref, attn_ref, conv_ref, woa_ref, woc_ref, gpm_ref, gpf_ref, gqf_ref,
                    wg_ref, wu_ref, wd_ref, o_ref, *, d_ff):
    mixed = jnp.dot(conv_ref[...], woc_ref[...], preferred_element_type=F32)
    for hd in range(N_HEADS):
        mixed = mixed + jnp.dot(attn_ref[hd], woa_ref[hd], preferred_element_type=F32)
    x1 = x_ref[...] + _rmsnorm(mixed, gpm_ref[...])
    h = _rmsnorm(x1, gpf_ref[...]).astype(BF16)
    ff = jnp.zeros(x1.shape, F32)
    for c in range(d_ff // FF_CHUNK):
        c0 = c * FF_CHUNK
        gate = jnp.dot(h, wg_ref[:, c0:c0 + FF_CHUNK], preferred_element_type=F32)
        up = jnp.dot(h, wu_ref[:, c0:c0 + FF_CHUNK], preferred_element_type=F32)
        act = (gate * _sigmoid(gate) * up).astype(BF16)
        ff = ff + jnp.dot(act, wd_ref[c0:c0 + FF_CHUNK, :], preferred_element_type=F32)
    o_ref[...] = x1 + _rmsnorm(ff, gqf_ref[...])


def _out_ffn(x, attn, conv, woa, woc, gpm, gpf, gqf, wg, wu, wd, tm):
    b, tp, d = x.shape
    d_ff = wg.shape[1]
    row = lambda width: pl.BlockSpec((None, tm, width), lambda bi, i: (bi, i, 0))
    tiles = pl.BlockSpec((None, N_HEADS, tm, LANES), lambda bi, i: (bi, 0, i, 0))
    consts = (woa, woc, gpm, gpf, gqf, wg, wu, wd)
    return pl.pallas_call(
        functools.partial(_out_ffn_kernel, d_ff=d_ff),
        grid=(b, tp // tm),
        in_specs=[row(d), tiles, row(conv.shape[2])] + [_resident(c.shape) for c in consts],
        out_specs=row(d),
        out_shape=jax.ShapeDtypeStruct((b, tp, d), F32),
        compiler_params=pltpu.CompilerParams(
            dimension_semantics=("arbitrary", "arbitrary"), vmem_limit_bytes=VMEM_LIMIT),
        name="out_ffn",
    )(x, attn, conv, *consts)


def _sample_in_kernel(x_ref, g_ref, w_ref, bf_ref, st_ref, wdw_ref, bdw_ref, gln_ref, bln_ref,
                      q_ref, k_ref, v_ref, lf_ref, conv_ref, ns_ref, *, conv_k):
    ht = HEAD_TILES
    cw = conv_ref.shape[1]
    h = _rmsnorm(x_ref[...], g_ref[...]).astype(BF16)
    u = jnp.dot(h, w_ref[...], preferred_element_type=F32)
    q_ref[...] = _compact_heads(u[:, 0:ht])
    k_ref[...] = _compact_heads(u[:, ht:2 * ht])
    v_ref[...] = _compact_heads(u[:, 2 * ht:3 * ht])
    fg0 = 3 * ht + 2 * cw
    lf_ref[...] = _log_sigmoid(u[:, fg0:fg0 + LANES] + bf_ref[...])[:, 0:N_HEADS]
    glu = u[:, 3 * ht:3 * ht + cw] * _sigmoid(u[:, 3 * ht + cw:3 * ht + 2 * cw])
    acc = bdw_ref[...] + wdw_ref[conv_k - 1:conv_k, :] * glu
    for j in range(conv_k - 1):
        acc = acc + wdw_ref[j:j + 1, :] * st_ref[j]
    conv_ref[...] = _conv_ln_silu(acc, gln_ref[...], bln_ref[...]).astype(BF16)
    for j in range(conv_k - 2):
        ns_ref[j] = st_ref[j + 1]
    ns_ref[conv_k - 2] = glu


def _sample_in(x, g, w, bfg, state_t, layer, wdw, bdw, gln, bln):
    n, d = x.shape
    a = ATTN_WIDTH
    cw = wdw.shape[1]
    conv_k = wdw.shape[0]
    full = lambda arr: pl.BlockSpec(arr.shape, lambda i: (0,) * arr.ndim)
    out = lambda width, dt: (jax.ShapeDtypeStruct((n, width), dt), pl.BlockSpec((n, width), lambda i: (0, 0)))
    outs = [out(a, F32), out(a, F32), out(a, F32), out(N_HEADS, F32), out(cw, BF16),
            (jax.ShapeDtypeStruct(state_t.shape[1:], F32),
             pl.BlockSpec(state_t.shape[1:], lambda i: (0, 0, 0)))]
    st_spec = pl.BlockSpec((None,) + state_t.shape[1:], lambda i: (layer, 0, 0, 0))
    return pl.pallas_call(
        functools.partial(_sample_in_kernel, conv_k=conv_k),
        grid=(1,),
        in_specs=[full(x), full(g), full(w), full(bfg), st_spec, full(wdw), full(bdw), full(gln), full(bln)],
        out_specs=tuple(o[1] for o in outs),
        out_shape=tuple(o[0] for o in outs),
        compiler_params=pltpu.CompilerParams(
            dimension_semantics=("arbitrary",), vmem_limit_bytes=VMEM_LIMIT),
        name="sample_in",
    )(x, g, w, bfg, state_t, wdw, bdw, gln, bln)


def _sample_attn_kernel(pt_ref, q_ref, kn_ref, vn_ref, lfn_ref, us_ref, *rest, pps):
    k_refs = rest[0:pps]
    v_refs = rest[pps:2 * pps]
    x_refs = rest[2 * pps:3 * pps]
    o_ref = rest[3 * pps]
    qb_ref, m_ref, l_ref, acc_ref, carry_ref = rest[3 * pps + 1:]
    g = pl.program_id(1)
    page = qb_ref.shape[2]

    @pl.when(g == 0)
    def _():
        for hd in range(N_HEADS):
            qb_ref[hd] = jnp.broadcast_to(q_ref[hd], (HEAD_DIM, page))
        m_ref[...] = jnp.full(m_ref.shape, NEG_INF, F32)
        l_ref[...] = jnp.zeros_like(l_ref)
        acc_ref[...] = jnp.zeros_like(acc_ref)
        carry_ref[...] = jnp.zeros_like(carry_ref)

    us = us_ref[...]
    lfn = lfn_ref[...]

    for idx in range(pps):
        hi, mid, lo = _split3(x_refs[idx][...])
        yr = (jnp.dot(hi, us, preferred_element_type=F32)
              + jnp.dot(mid, us, preferred_element_type=F32)
              + jnp.dot(lo, us, preferred_element_type=F32))
        carry = carry_ref[...]
        bias = yr[:, 0:page] + carry + lfn
        carry_ref[...] = carry + yr[:, page:2 * page]
        for hd in range(N_HEADS):
            s = jnp.sum(qb_ref[hd] * k_refs[idx][hd], axis=0, keepdims=True) + bias[hd:hd + 1, :]
            m_prev = m_ref[hd:hd + 1, :]
            m_new = jnp.maximum(m_prev, s)
            alpha = jnp.exp(m_prev - m_new)
            p = jnp.exp(s - m_new)
            l_ref[hd:hd + 1, :] = alpha * l_ref[hd:hd + 1, :] + p
            m_ref[hd:hd + 1, :] = m_new
            acc_ref[hd] = alpha * acc_ref[hd] + p * v_refs[idx][hd]

    @pl.when(g == pl.num_programs(1) - 1)
    def _():
        for hd in range(N_HEADS):
            m = m_ref[hd:hd + 1, :]
            m_all = jnp.max(m, axis=1, keepdims=True)
            w = jnp.exp(m - m_all)
            l_all = jnp.sum(l_ref[hd:hd + 1, :] * w, axis=1, keepdims=True)
            o_all = jnp.sum(acc_ref[hd] * w, axis=1, keepdims=True)
            s_new = jnp.sum(q_ref[hd] * kn_ref[hd], axis=0, keepdims=True)
            m_fin = jnp.maximum(m_all, s_new)
            a_old = jnp.exp(m_all - m_fin)
            p_new = jnp.exp(s_new - m_fin)
            o_ref[hd] = (a_old * o_all + p_new * vn_ref[hd]) / (a_old * l_all + p_new)


def _sample_attn(page_table, q, kn, vn, lfn, us, cache_kt, cache_vt, cache_xt, layer):
    n, n_pages = page_table.shape
    page = cache_kt.shape[4]
    pps = PAGES_PER_STEP
    steps = n_pages // pps
    per_seq = lambda arr: pl.BlockSpec((None,) + arr.shape[1:], lambda bi, g, pt: (bi,) + (0,) * (arr.ndim - 1))

    def paged(arr, idx):
        zeros = (0,) * (arr.ndim - 2)
        return pl.BlockSpec(
            (None, None) + arr.shape[2:],
            lambda bi, g, pt: (layer, pt[bi, n_pages - 1 - (g * pps + idx)]) + zeros)

    in_specs = [per_seq(q), per_seq(kn), per_seq(vn), per_seq(lfn),
                pl.BlockSpec(us.shape, lambda bi, g, pt: (0, 0))]
    in_specs += [paged(cache_kt, idx) for idx in range(pps)]
    in_specs += [paged(cache_vt, idx) for idx in range(pps)]
    in_specs += [paged(cache_xt, idx) for idx in range(pps)]
    grid_spec = pltpu.PrefetchScalarGridSpec(
        num_scalar_prefetch=1,
        grid=(n, steps),
        in_specs=in_specs,
        out_specs=per_seq(q),
        scratch_shapes=[
            pltpu.VMEM((N_HEADS, HEAD_DIM, page), F32), pltpu.VMEM((N_HEADS, page), F32),
            pltpu.VMEM((N_HEADS, page), F32), pltpu.VMEM((N_HEADS, HEAD_DIM, page), F32),
            pltpu.VMEM((N_HEADS, page), F32)],
    )
    args = [page_table, q, kn, vn, lfn, us] + [cache_kt] * pps + [cache_vt] * pps + [cache_xt] * pps
    return pl.pallas_call(
        functools.partial(_sample_attn_kernel, pps=pps),
        grid_spec=grid_spec,
        out_shape=jax.ShapeDtypeStruct(q.shape, F32),
        compiler_params=pltpu.CompilerParams(
            dimension_semantics=("arbitrary", "arbitrary"), vmem_limit_bytes=VMEM_LIMIT),
        name="sample_attn",
    )(*args)


def _suffix_matrix(page):
    src = np.arange(page)[:, None]
    dst = np.arange(page)[None, :]
    after = src > dst
    return jnp.asarray(np.concatenate([after, np.ones_like(after)], axis=1).astype(np.float32), BF16)


def _aux_constants():
    eq = np.zeros((3 * LANES, HEAD_TILES), np.float32)
    ek = np.zeros((3 * LANES, HEAD_TILES), np.float32)
    cq = np.zeros((1, HEAD_TILES), np.float32)
    ck = np.zeros((1, HEAD_TILES), np.float32)
    cv = np.zeros((1, HEAD_TILES), np.float32)
    for hd in range(N_HEADS):
        base = hd * LANES + _aux_base(hd)
        for part in range(3):
            eq[part * LANES + hd, base + part] = 1.0
            ek[part * LANES + hd, base + 3 + part] = -1.0
            cq[0, base + 3 + part] = 1.0
            ck[0, base + part] = 1.0
        cv[0, base] = 1.0
    return (jnp.asarray(eq, BF16), jnp.asarray(ek, BF16), jnp.asarray(cq), jnp.asarray(ck), jnp.asarray(cv))


def _head_tiles(w, axis):
    w = jnp.moveaxis(w, axis, -1)
    lead = w.shape[:-1]
    w4 = w.reshape(lead + (N_HEADS // 2, 2, HEAD_DIM))
    z = jnp.zeros(lead + (N_HEADS // 2, HEAD_DIM), w.dtype)
    even = jnp.concatenate([w4[..., 0, :], z], axis=-1)
    odd = jnp.concatenate([z, w4[..., 1, :]], axis=-1)
    out = jnp.stack([even, odd], axis=-2).reshape(lead + (HEAD_TILES,))
    return jnp.moveaxis(out, -1, axis)


def kernel(x_prompt, x_sample, cache_k, cache_v, cache_logf, state_conv, page_table, meta_tokens,
           w_in, b_forget, w_dw, b_dw, g_conv_ln, b_conv_ln, w_out, g_pre_mix, g_post_mix,
           g_pre_ffn, g_post_ffn, w_gate, w_up, w_down):
    depth = w_in.shape[0]
    b, seq, d = x_prompt.shape
    n = x_sample.shape[0]
    a = ATTN_WIDTH
    cw = w_dw.shape[2]
    page = cache_k.shape[2]
    t_real = N_META + seq
    tp = -(-t_real // ATTN_TILE) * ATTN_TILE

    scale = HEAD_DIM ** -0.5
    fg0 = 3 * a
    ca0 = fg0 + N_HEADS
    w_in_r = jnp.concatenate([
        _head_tiles(w_in[:, :, 0:a] * scale, 2), _head_tiles(w_in[:, :, a:2 * a], 2),
        _head_tiles(w_in[:, :, 2 * a:3 * a], 2), w_in[:, :, ca0:ca0 + 2 * cw],
        jnp.pad(w_in[:, :, fg0:ca0], ((0, 0), (0, 0), (0, LANES - N_HEADS)))], axis=2).astype(BF16)
    bfg = jnp.pad(b_forget, ((0, 0), (0, LANES - N_HEADS)))[:, None, :]
    w_out_a = _head_tiles(w_out[:, 0:a, :], 1).astype(BF16).reshape(depth, N_HEADS, LANES, d)
    w_out_c = w_out[:, a:, :].astype(BF16)
    w_gate_b = w_gate.astype(BF16)
    w_up_b = w_up.astype(BF16)
    w_down_b = w_down.astype(BF16)
    vec = lambda p: p[:, None, :]
    tri = jnp.asarray(np.tril(np.ones((ROW_TILE, ROW_TILE), np.float32)), BF16)
    aux = _aux_constants()
    us = _suffix_matrix(page)
    cache_kt = jnp.transpose(cache_k, (0, 1, 3, 4, 2))
    cache_vt = jnp.transpose(cache_v, (0, 1, 3, 4, 2))
    cache_xt = jnp.transpose(cache_logf, (0, 1, 3, 2))
    state_t = jnp.transpose(state_conv, (0, 2, 1, 3))

    meta = jnp.broadcast_to(meta_tokens[None].astype(x_prompt.dtype), (b, N_META, d))
    xp = jnp.concatenate([meta, x_prompt, jnp.zeros((b, tp - t_real, d), x_prompt.dtype)], axis=1)
    xs = x_sample.reshape(n, d)

    kp, vp, lp, cp, ks, vs, ls, cs = ([] for _ in range(8))
    for l in range(depth):
        lw_in = (vec(g_pre_mix)[l], w_in_r[l], bfg[l])
        lw_conv = (w_dw[l], vec(b_dw)[l], vec(g_conv_ln)[l], vec(b_conv_ln)[l])
        lw_out = (w_out_a[l], w_out_c[l], vec(g_post_mix)[l], vec(g_pre_ffn)[l],
                  vec(g_post_ffn)[l], w_gate_b[l], w_up_b[l], w_down_b[l])

        qt, kt, vt, kf, vf, lf, conv, cst = _prompt_in(xp, *lw_in, *lw_conv, tri, *aux, t_real)
        attn = _prompt_attn(qt, kt, vt)
        xp = _out_ffn(xp, attn, conv, *lw_out, tm=ROW_TILE)
        kp.append(kf); vp.append(vf); lp.append(lf); cp.append(cst)

        qs, k_new, v_new, lf_new, conv_s, new_state = _sample_in(xs, *lw_in, state_t, l, *lw_conv)
        col = lambda t: t.reshape(n, N_HEADS, HEAD_DIM, 1)
        o = _sample_attn(page_table, col(qs), col(k_new), col(v_new), lf_new[:, :, None], us,
                         cache_kt, cache_vt, cache_xt, l)
        o_tiles = _head_tiles(o.reshape(n, a), 1).reshape(n, N_HEADS, LANES).transpose(1, 0, 2)
        xs = _out_ffn(xs[None], o_tiles[None].astype(BF16), conv_s[None], *lw_out, tm=n)[0]
        ks.append(k_new); vs.append(v_new); ls.append(lf_new); cs.append(new_state)

    hd = (N_HEADS, HEAD_DIM)
    return (xp[:, N_META:t_real], xs[:, None, :],
            jnp.stack(kp).reshape(depth, b, t_real, *hd), jnp.stack(vp).reshape(depth, b, t_real, *hd),
            jnp.stack(lp), jnp.stack(cp),
            jnp.stack(ks).reshape(depth, n, 1, *hd), jnp.stack(vs).reshape(depth, n, 1, *hd),
            jnp.stack(ls)[:, :, None, :], jnp.transpose(jnp.stack(cs), (0, 2, 1, 3)))
```

```python
import functools

import numpy as np
import jax
import jax.numpy as jnp
from jax import lax
from jax.experimental import pallas as pl
from jax.experimental.pallas import tpu as pltpu

N_HEADS = 8
HEAD_DIM = 64
ATTN_WIDTH = N_HEADS * HEAD_DIM
N_META = 16
EPS = 1e-6
NEG_INF = -1e30

LANES = 128
SUBLANES = 8
HEAD_TILES = N_HEADS * LANES
ATTN_TILE = 768
ROW_TILE = 384
OUT_TILE = 768
CONV_CHUNK = 64
HIST_ROWS = 32
FF_CHUNK = 256
PAGES_PER_STEP = 8
VMEM_LIMIT = 56 * 1024 * 1024

F32 = jnp.float32
BF16 = jnp.bfloat16


def _aux_base(head):
    return HEAD_DIM if head % 2 == 0 else 0


def _resident(shape):
    zeros = (0,) * len(shape)
    return pl.BlockSpec(shape, lambda *_: zeros, pipeline_mode=pl.Buffered(1))


def _rmsnorm(x, g):
    return x * lax.rsqrt(jnp.mean(x * x, axis=-1, keepdims=True) + EPS) * g


def _log_sigmoid(x):
    return jnp.minimum(x, 0.0) - jnp.log1p(jnp.exp(-jnp.abs(x)))


def _sigmoid(x):
    return 1.0 / (1.0 + jnp.exp(-x))


def _split3(x):
    hi = x.astype(BF16)
    r1 = x - hi.astype(F32)
    mid = r1.astype(BF16)
    lo = (r1 - mid.astype(F32)).astype(BF16)
    return hi, mid, lo


def _conv_ln_silu(acc, g_ln, b_ln):
    mu = jnp.mean(acc, axis=-1, keepdims=True)
    xc = acc - mu
    y = xc * lax.rsqrt(jnp.mean(xc * xc, axis=-1, keepdims=True) + EPS) * g_ln + b_ln
    return y * _sigmoid(y)


def _compact_heads(tiles):
    pairs = [tiles[:, (2 * p) * LANES:(2 * p + 1) * LANES] + tiles[:, (2 * p + 1) * LANES:(2 * p + 2) * LANES]
             for p in range(N_HEADS // 2)]
    return jnp.concatenate(pairs, axis=1)


def _prompt_in_kernel(x_ref, g_ref, w_ref, bf_ref, wdw_ref, bdw_ref, gln_ref, bln_ref, tri_ref,
                      eq_ref, ek_ref, cq_ref, ck_ref, cv_ref,
                      q_ref, k_ref, v_ref, kf_ref, vf_ref, lf_ref, conv_ref, cst_ref,
                      ext_ref, sh_ref, carry_ref, *, tm, n_tiles, t_real, conv_k):
    i = pl.program_id(1)
    ht = HEAD_TILES
    cw = conv_ref.shape[1]

    @pl.when(i == 0)
    def _():
        ext_ref[0:HIST_ROWS, :] = jnp.zeros((HIST_ROWS, cw), F32)
        carry_ref[...] = jnp.zeros_like(carry_ref)

    h = _rmsnorm(x_ref[...], g_ref[...]).astype(BF16)
    u = jnp.dot(h, w_ref[...], preferred_element_type=F32)
    uq = u[:, 0:ht]
    uk = u[:, ht:2 * ht]
    uv = u[:, 2 * ht:3 * ht]
    kf_ref[...] = _compact_heads(uk)
    vf_ref[...] = _compact_heads(uv)

    fg0 = 3 * ht + 2 * cw
    logf = _log_sigmoid(u[:, fg0:fg0 + LANES] + bf_ref[...])
    lf_ref[...] = logf[:, 0:N_HEADS]
    tri = tri_ref[...]
    hi, mid, lo = _split3(logf)
    f = (jnp.dot(tri, hi, preferred_element_type=F32)
         + jnp.dot(tri, mid, preferred_element_type=F32)
         + jnp.dot(tri, lo, preferred_element_type=F32)) + carry_ref[...]
    carry_ref[...] = f[tm - 1:tm, :]
    fparts = jnp.concatenate(_split3(f), axis=1)
    qt = (uq + jnp.dot(fparts, eq_ref[...], preferred_element_type=F32) + cq_ref[...]).astype(BF16)
    kt = (uk + jnp.dot(fparts, ek_ref[...], preferred_element_type=F32) + ck_ref[...]).astype(BF16)
    vt = uv + cv_ref[...]
    for hd in range(N_HEADS):
        q_ref[hd] = qt[:, hd * LANES:(hd + 1) * LANES]
        k_ref[hd] = kt[:, hd * LANES:(hd + 1) * LANES]
        v_ref[hd] = vt[:, hd * LANES:(hd + 1) * LANES].T.astype(BF16)

    ca = u[:, 3 * ht:3 * ht + cw]
    cb = u[:, 3 * ht + cw:3 * ht + 2 * cw]
    ext_ref[HIST_ROWS:HIST_ROWS + tm, :] = ca * _sigmoid(cb)
    base = HIST_ROWS - (conv_k - 1)
    span = sh_ref.shape[1]
    for r in range(1, SUBLANES):
        sh_ref[r - 1] = ext_ref[r:r + span, :]
    bdw = bdw_ref[...]
    gln = gln_ref[...]
    bln = bln_ref[...]
    for c in range(tm // CONV_CHUNK):
        r0 = c * CONV_CHUNK
        acc = jnp.broadcast_to(bdw, (CONV_CHUNK, cw))
        for j in range(conv_k):
            r = (base + j) % SUBLANES
            a0 = r0 + base + j - r
            rows = ext_ref[a0:a0 + CONV_CHUNK, :] if r == 0 else sh_ref[r - 1, a0:a0 + CONV_CHUNK, :]
            acc = acc + wdw_ref[j:j + 1, :] * rows
        conv_ref[r0:r0 + CONV_CHUNK, :] = _conv_ln_silu(acc, gln, bln).astype(BF16)

    last = (t_real - 1) // tm

    @pl.when(i == last)
    def _():
        off = HIST_ROWS + (t_real - (conv_k - 1) - last * tm)
        cst_ref[...] = ext_ref[off:off + conv_k - 1, :]

    ext_ref[0:HIST_ROWS, :] = ext_ref[tm:tm + HIST_ROWS, :]


def _prompt_in(x, g, w, bfg, wdw, bdw, gln, bln, tri, eq, ek, cq, ck, cv, t_real):
    b, tp, d = x.shape
    tm = ROW_TILE
    n_tiles = tp // tm
    a = ATTN_WIDTH
    cw = wdw.shape[1]
    conv_k = wdw.shape[0]
    assert conv_k - 1 <= HIST_ROWS and t_real - ((t_real - 1) // tm) * tm >= conv_k - 1
    row = lambda width: pl.BlockSpec((None, tm, width), lambda bi, i: (bi, i, 0))
    tiles = pl.BlockSpec((None, N_HEADS, tm, LANES), lambda bi, i: (bi, 0, i, 0))
    kernel = functools.partial(_prompt_in_kernel, tm=tm, n_tiles=n_tiles, t_real=t_real, conv_k=conv_k)
    out_shape = (
        jax.ShapeDtypeStruct((b, N_HEADS, tp, LANES), BF16),
        jax.ShapeDtypeStruct((b, N_HEADS, tp, LANES), BF16),
        jax.ShapeDtypeStruct((b, N_HEADS, LANES, tp), BF16),
        jax.ShapeDtypeStruct((b, t_real, a), F32),
        jax.ShapeDtypeStruct((b, t_real, a), F32),
        jax.ShapeDtypeStruct((b, t_real, N_HEADS), F32),
        jax.ShapeDtypeStruct((b, tp, cw), BF16),
        jax.ShapeDtypeStruct((b, conv_k - 1, cw), F32),
    )
    tiles_t = pl.BlockSpec((None, N_HEADS, LANES, tm), lambda bi, i: (bi, 0, 0, i))
    out_specs = (tiles, tiles, tiles_t, row(a), row(a), row(N_HEADS), row(cw),
                 pl.BlockSpec((None, conv_k - 1, cw), lambda bi, i: (bi, 0, 0)))
    consts = (g, w, bfg, wdw, bdw, gln, bln, tri, eq, ek, cq, ck, cv)
    return pl.pallas_call(
        kernel,
        grid=(b, n_tiles),
        in_specs=[row(d)] + [_resident(c.shape) for c in consts],
        out_specs=out_specs,
        out_shape=out_shape,
        scratch_shapes=[pltpu.VMEM((tm + HIST_ROWS, cw), F32),
                        pltpu.VMEM((SUBLANES - 1, tm + HIST_ROWS - SUBLANES, cw), F32),
                        pltpu.VMEM((1, LANES), F32)],
        compiler_params=pltpu.CompilerParams(
            dimension_semantics=("arbitrary", "arbitrary"), vmem_limit_bytes=VMEM_LIMIT),
        name="prompt_in",
    )(x, *consts)


def _prompt_attn_kernel(q_ref, k_ref, v_ref, o_ref, sa_ref, sb_ref, m_ref, acc_ref, *, tq):
    i = pl.program_id(2)
    m_ref[...] = jnp.full(m_ref.shape, NEG_INF, F32)
    acc_ref[...] = jnp.zeros_like(acc_ref)

    def scores(j, s_ref):
        start = pl.multiple_of(j * tq, tq)
        for hh in range(2):
            s_ref[hh] = lax.dot_general(k_ref[hh, pl.ds(start, tq), :], q_ref[hh],
                                        (((1,), (1,)), ((), ())), preferred_element_type=F32)

    def consume(j, s_ref, masked):
        start = pl.multiple_of(j * tq, tq)
        for hh in range(2):
            s = s_ref[hh]
            if masked:
                key = lax.broadcasted_iota(jnp.int32, (tq, tq), 0)
                qry = lax.broadcasted_iota(jnp.int32, (tq, tq), 1)
                s = jnp.where(key <= qry, s, NEG_INF)
            m_prev = m_ref[hh]
            m_new = jnp.maximum(m_prev, jnp.max(s, axis=0, keepdims=True))
            alpha = jnp.exp(m_prev - m_new)
            p = jnp.exp(s - m_new)
            acc_ref[hh] = alpha * acc_ref[hh] + jnp.dot(
                v_ref[hh, :, pl.ds(start, tq)], p.astype(BF16), preferred_element_type=F32)
            m_ref[hh] = m_new

    scores(0, sa_ref)

    def body(t, carry):
        scores(2 * t + 1, sb_ref)
        consume(2 * t, sa_ref, False)
        scores(2 * t + 2, sa_ref)
        consume(2 * t + 1, sb_ref, False)
        return carry

    lax.fori_loop(0, i // 2, body, 0)

    @pl.when(i % 2 == 0)
    def _():
        consume(i, sa_ref, True)

    @pl.when(i % 2 == 1)
    def _():
        scores(i, sb_ref)
        consume(i - 1, sa_ref, False)
        consume(i, sb_ref, True)

    outs = []
    for hh in range(2):
        acc = acc_ref[hh]
        ab = _aux_base(hh)
        outs.append((acc / acc[ab:ab + 1, :]).T)
    lane = lax.broadcasted_iota(jnp.int32, (tq, LANES), 1)
    o_ref[...] = jnp.where(lane < HEAD_DIM, outs[0], outs[1]).astype(o_ref.dtype)


def _prompt_attn(q, k, vt):
    b, nh, tp, _ = q.shape
    tq = ATTN_TILE
    qspec = pl.BlockSpec((None, 2, tq, LANES), lambda bi, hp, i: (bi, hp, i, 0))
    kspec = pl.BlockSpec((None, 2, tp, LANES), lambda bi, hp, i: (bi, hp, 0, 0))
    vspec = pl.BlockSpec((None, 2, LANES, tp), lambda bi, hp, i: (bi, hp, 0, 0))
    return pl.pallas_call(
        functools.partial(_prompt_attn_kernel, tq=tq),
        grid=(b, nh // 2, tp // tq),
        in_specs=[qspec, kspec, vspec],
        out_specs=pl.BlockSpec((None, tq, LANES), lambda bi, hp, i: (bi, i, hp)),
        out_shape=jax.ShapeDtypeStruct((b, tp, ATTN_WIDTH), BF16),
        scratch_shapes=[pltpu.VMEM((2, tq, tq), F32), pltpu.VMEM((2, tq, tq), F32),
                        pltpu.VMEM((2, 1, tq), F32), pltpu.VMEM((2, LANES, tq), F32)],
        compiler_params=pltpu.CompilerParams(
            dimension_semantics=("arbitrary", "arbitrary", "arbitrary"), vmem_limit_bytes=VMEM_LIMIT),
        name="prompt_attn",
    )(q, k, vt)


def _out_ffn_kernel(x_ref, attn_ref, conv_ref, woa_ref, woc_ref, gpm_ref, gpf_ref, gqf_ref,
                    wg_ref, wu_ref, wd_ref, o_ref, *, d_ff):
    mixed = (jnp.dot(attn_ref[...], woa_ref[...], preferred_element_type=F32)
             + jnp.dot(conv_ref[...], woc_ref[...], preferred_element_type=F32))
    x1 = x_ref[...] + _rmsnorm(mixed, gpm_ref[...])
    h = _rmsnorm(x1, gpf_ref[...]).astype(BF16)
    ff = jnp.zeros(x1.shape, F32)
    for c in range(d_ff // FF_CHUNK):
        c0 = c * FF_CHUNK
        gate = jnp.dot(h, wg_ref[:, c0:c0 + FF_CHUNK], preferred_element_type=F32)
        up = jnp.dot(h, wu_ref[:, c0:c0 + FF_CHUNK], preferred_element_type=F32)
        act = (gate * _sigmoid(gate) * up).astype(BF16)
        ff = ff + jnp.dot(act, wd_ref[c0:c0 + FF_CHUNK, :], preferred_element_type=F32)
    o_ref[...] = x1 + _rmsnorm(ff, gqf_ref[...])


def _out_ffn(x, attn, conv, woa, woc, gpm, gpf, gqf, wg, wu, wd, tm):
    b, tp, d = x.shape
    d_ff = wg.shape[1]
    row = lambda width: pl.BlockSpec((None, tm, width), lambda bi, i: (bi, i, 0))
    consts = (woa, woc, gpm, gpf, gqf, wg, wu, wd)
    return pl.pallas_call(
        functools.partial(_out_ffn_kernel, d_ff=d_ff),
        grid=(b, tp // tm),
        in_specs=[row(d), row(attn.shape[2]), row(conv.shape[2])] + [_resident(c.shape) for c in consts],
        out_specs=row(d),
        out_shape=jax.ShapeDtypeStruct((b, tp, d), F32),
        compiler_params=pltpu.CompilerParams(
            dimension_semantics=("arbitrary", "arbitrary"), vmem_limit_bytes=VMEM_LIMIT),
        name="out_ffn",
    )(x, attn, conv, *consts)


def _sample_in_kernel(x_ref, g_ref, w_ref, bf_ref, st_ref, wdw_ref, bdw_ref, gln_ref, bln_ref,
                      q_ref, k_ref, v_ref, lf_ref, conv_ref, ns_ref, *, conv_k):
    ht = HEAD_TILES
    cw = conv_ref.shape[1]
    h = _rmsnorm(x_ref[...], g_ref[...]).astype(BF16)
    u = jnp.dot(h, w_ref[...], preferred_element_type=F32)
    q_ref[...] = _compact_heads(u[:, 0:ht])
    k_ref[...] = _compact_heads(u[:, ht:2 * ht])
    v_ref[...] = _compact_heads(u[:, 2 * ht:3 * ht])
    fg0 = 3 * ht + 2 * cw
    lf_ref[...] = _log_sigmoid(u[:, fg0:fg0 + LANES] + bf_ref[...])[:, 0:N_HEADS]
    glu = u[:, 3 * ht:3 * ht + cw] * _sigmoid(u[:, 3 * ht + cw:3 * ht + 2 * cw])
    acc = bdw_ref[...] + wdw_ref[conv_k - 1:conv_k, :] * glu
    for j in range(conv_k - 1):
        acc = acc + wdw_ref[j:j + 1, :] * st_ref[j]
    conv_ref[...] = _conv_ln_silu(acc, gln_ref[...], bln_ref[...]).astype(BF16)
    for j in range(conv_k - 2):
        ns_ref[j] = st_ref[j + 1]
    ns_ref[conv_k - 2] = glu


def _sample_in(x, g, w, bfg, state_t, layer, wdw, bdw, gln, bln):
    n, d = x.shape
    a = ATTN_WIDTH
    cw = wdw.shape[1]
    conv_k = wdw.shape[0]
    full = lambda arr: pl.BlockSpec(arr.shape, lambda i: (0,) * arr.ndim)
    out = lambda width, dt: (jax.ShapeDtypeStruct((n, width), dt), pl.BlockSpec((n, width), lambda i: (0, 0)))
    outs = [out(a, F32), out(a, F32), out(a, F32), out(N_HEADS, F32), out(cw, BF16),
            (jax.ShapeDtypeStruct(state_t.shape[1:], F32),
             pl.BlockSpec(state_t.shape[1:], lambda i: (0, 0, 0)))]
    st_spec = pl.BlockSpec((None,) + state_t.shape[1:], lambda i: (layer, 0, 0, 0))
    return pl.pallas_call(
        functools.partial(_sample_in_kernel, conv_k=conv_k),
        grid=(1,),
        in_specs=[full(x), full(g), full(w), full(bfg), st_spec, full(wdw), full(bdw), full(gln), full(bln)],
        out_specs=tuple(o[1] for o in outs),
        out_shape=tuple(o[0] for o in outs),
        compiler_params=pltpu.CompilerParams(
            dimension_semantics=("arbitrary",), vmem_limit_bytes=VMEM_LIMIT),
        name="sample_in",
    )(x, g, w, bfg, state_t, wdw, bdw, gln, bln)


def _sample_attn_kernel(pt_ref, q_ref, kn_ref, vn_ref, lfn_ref, us_ref, *rest, pps):
    k_refs = rest[0:pps]
    v_refs = rest[pps:2 * pps]
    x_refs = rest[2 * pps:3 * pps]
    o_ref = rest[3 * pps]
    qb_ref, m_ref, l_ref, acc_ref, carry_ref = rest[3 * pps + 1:]
    g = pl.program_id(1)
    page = qb_ref.shape[2]

    @pl.when(g == 0)
    def _():
        for hd in range(N_HEADS):
            qb_ref[hd] = jnp.broadcast_to(q_ref[hd], (HEAD_DIM, page))
        m_ref[...] = jnp.full(m_ref.shape, NEG_INF, F32)
        l_ref[...] = jnp.zeros_like(l_ref)
        acc_ref[...] = jnp.zeros_like(acc_ref)
        carry_ref[...] = jnp.zeros_like(carry_ref)

    us = us_ref[...]
    lfn = lfn_ref[...]

    for idx in range(pps):
        hi, mid, lo = _split3(x_refs[idx][...])
        yr = (jnp.dot(hi, us, preferred_element_type=F32)
              + jnp.dot(mid, us, preferred_element_type=F32)
              + jnp.dot(lo, us, preferred_element_type=F32))
        carry = carry_ref[...]
        bias = yr[:, 0:page] + carry + lfn
        carry_ref[...] = carry + yr[:, page:2 * page]
        s = bias + jnp.concatenate(
            [jnp.sum(qb_ref[hd] * k_refs[idx][hd], axis=0, keepdims=True) for hd in range(N_HEADS)], axis=0)
        m_prev = m_ref[...]
        m_new = jnp.maximum(m_prev, s)
        alpha = jnp.exp(m_prev - m_new)
        p = jnp.exp(s - m_new)
        l_ref[...] = alpha * l_ref[...] + p
        m_ref[...] = m_new
        for hd in range(N_HEADS):
            acc_ref[hd] = alpha[hd:hd + 1, :] * acc_ref[hd] + p[hd:hd + 1, :] * v_refs[idx][hd]

    @pl.when(g == pl.num_programs(1) - 1)
    def _():
        for hd in range(N_HEADS):
            m = m_ref[hd:hd + 1, :]
            m_all = jnp.max(m, axis=1, keepdims=True)
            w = jnp.exp(m - m_all)
            l_all = jnp.sum(l_ref[hd:hd + 1, :] * w, axis=1, keepdims=True)
            o_all = jnp.sum(acc_ref[hd] * w, axis=1, keepdims=True)
            s_new = jnp.sum(q_ref[hd] * kn_ref[hd], axis=0, keepdims=True)
            m_fin = jnp.maximum(m_all, s_new)
            a_old = jnp.exp(m_all - m_fin)
            p_new = jnp.exp(s_new - m_fin)
            o_ref[hd] = (a_old * o_all + p_new * vn_ref[hd]) / (a_old * l_all + p_new)


def _sample_attn(page_table, q, kn, vn, lfn, us, cache_kt, cache_vt, cache_xt, layer):
    n, n_pages = page_table.shape
    page = cache_kt.shape[4]
    pps = PAGES_PER_STEP
    steps = n_pages // pps
    per_seq = lambda arr: pl.BlockSpec((None,) + arr.shape[1:], lambda bi, g, pt: (bi,) + (0,) * (arr.ndim - 1))

    def paged(arr, idx):
        zeros = (0,) * (arr.ndim - 2)
        return pl.BlockSpec(
            (None, None) + arr.shape[2:],
            lambda bi, g, pt: (layer, pt[bi, n_pages - 1 - (g * pps + idx)]) + zeros)

    in_specs = [per_seq(q), per_seq(kn), per_seq(vn), per_seq(lfn),
                pl.BlockSpec(us.shape, lambda bi, g, pt: (0, 0))]
    in_specs += [paged(cache_kt, idx) for idx in range(pps)]
    in_specs += [paged(cache_vt, idx) for idx in range(pps)]
    in_specs += [paged(cache_xt, idx) for idx in range(pps)]
    grid_spec = pltpu.PrefetchScalarGridSpec(
        num_scalar_prefetch=1,
        grid=(n, steps),
        in_specs=in_specs,
        out_specs=per_seq(q),
        scratch_shapes=[
            pltpu.VMEM((N_HEADS, HEAD_DIM, page), F32), pltpu.VMEM((N_HEADS, page), F32),
            pltpu.VMEM((N_HEADS, page), F32), pltpu.VMEM((N_HEADS, HEAD_DIM, page), F32),
            pltpu.VMEM((N_HEADS, page), F32)],
    )
    args = [page_table, q, kn, vn, lfn, us] + [cache_kt] * pps + [cache_vt] * pps + [cache_xt] * pps
    return pl.pallas_call(
        functools.partial(_sample_attn_kernel, pps=pps),
        grid_spec=grid_spec,
        out_shape=jax.ShapeDtypeStruct(q.shape, F32),
        compiler_params=pltpu.CompilerParams(
            dimension_semantics=("arbitrary", "arbitrary"), vmem_limit_bytes=VMEM_LIMIT),
        name="sample_attn",
    )(*args)


def _suffix_matrix(page):
    src = np.arange(page)[:, None]
    dst = np.arange(page)[None, :]
    after = src > dst
    return jnp.asarray(np.concatenate([after, np.ones_like(after)], axis=1).astype(np.float32), BF16)


def _aux_constants():
    eq = np.zeros((3 * LANES, HEAD_TILES), np.float32)
    ek = np.zeros((3 * LANES, HEAD_TILES), np.float32)
    cq = np.zeros((1, HEAD_TILES), np.float32)
    ck = np.zeros((1, HEAD_TILES), np.float32)
    cv = np.zeros((1, HEAD_TILES), np.float32)
    for hd in range(N_HEADS):
        base = hd * LANES + _aux_base(hd)
        for part in range(3):
            eq[part * LANES + hd, base + part] = 1.0
            ek[part * LANES + hd, base + 3 + part] = -1.0
            cq[0, base + 3 + part] = 1.0
            ck[0, base + part] = 1.0
        cv[0, base] = 1.0
    return (jnp.asarray(eq, BF16), jnp.asarray(ek, BF16), jnp.asarray(cq), jnp.asarray(ck), jnp.asarray(cv))


def _head_tiles(w, axis):
    w = jnp.moveaxis(w, axis, -1)
    lead = w.shape[:-1]
    w4 = w.reshape(lead + (N_HEADS // 2, 2, HEAD_DIM))
    z = jnp.zeros(lead + (N_HEADS // 2, HEAD_DIM), w.dtype)
    even = jnp.concatenate([w4[..., 0, :], z], axis=-1)
    odd = jnp.concatenate([z, w4[..., 1, :]], axis=-1)
    out = jnp.stack([even, odd], axis=-2).reshape(lead + (HEAD_TILES,))
    return jnp.moveaxis(out, -1, axis)


def kernel(x_prompt, x_sample, cache_k, cache_v, cache_logf, state_conv, page_table, meta_tokens,
           w_in, b_forget, w_dw, b_dw, g_conv_ln, b_conv_ln, w_out, g_pre_mix, g_post_mix,
           g_pre_ffn, g_post_ffn, w_gate, w_up, w_down):
    depth = w_in.shape[0]
    b, seq, d = x_prompt.shape
    n = x_sample.shape[0]
    a = ATTN_WIDTH
    cw = w_dw.shape[2]
    page = cache_k.shape[2]
    t_real = N_META + seq
    tp = -(-t_real // ATTN_TILE) * ATTN_TILE

    scale = HEAD_DIM ** -0.5
    fg0 = 3 * a
    ca0 = fg0 + N_HEADS
    w_in_r = jnp.concatenate([
        _head_tiles(w_in[:, :, 0:a] * scale, 2), _head_tiles(w_in[:, :, a:2 * a], 2),
        _head_tiles(w_in[:, :, 2 * a:3 * a], 2), w_in[:, :, ca0:ca0 + 2 * cw],
        jnp.pad(w_in[:, :, fg0:ca0], ((0, 0), (0, 0), (0, LANES - N_HEADS)))], axis=2).astype(BF16)
    bfg = jnp.pad(b_forget, ((0, 0), (0, LANES - N_HEADS)))[:, None, :]
    w_out_a = w_out[:, 0:a, :].astype(BF16)
    w_out_c = w_out[:, a:, :].astype(BF16)
    w_gate_b = w_gate.astype(BF16)
    w_up_b = w_up.astype(BF16)
    w_down_b = w_down.astype(BF16)
    vec = lambda p: p[:, None, :]
    tri = jnp.asarray(np.tril(np.ones((ROW_TILE, ROW_TILE), np.float32)), BF16)
    aux = _aux_constants()
    us = _suffix_matrix(page)
    cache_kt = jnp.transpose(cache_k, (0, 1, 3, 4, 2))
    cache_vt = jnp.transpose(cache_v, (0, 1, 3, 4, 2))
    cache_xt = jnp.transpose(cache_logf, (0, 1, 3, 2))
    state_t = jnp.transpose(state_conv, (0, 2, 1, 3))

    meta = jnp.broadcast_to(meta_tokens[None].astype(x_prompt.dtype), (b, N_META, d))
    xp = jnp.concatenate([meta, x_prompt, jnp.zeros((b, tp - t_real, d), x_prompt.dtype)], axis=1)
    xs = x_sample.reshape(n, d)

    kp, vp, lp, cp, ks, vs, ls, cs = ([] for _ in range(8))
    for l in range(depth):
        lw_in = (vec(g_pre_mix)[l], w_in_r[l], bfg[l])
        lw_conv = (w_dw[l], vec(b_dw)[l], vec(g_conv_ln)[l], vec(b_conv_ln)[l])
        lw_out = (w_out_a[l], w_out_c[l], vec(g_post_mix)[l], vec(g_pre_ffn)[l],
                  vec(g_post_ffn)[l], w_gate_b[l], w_up_b[l], w_down_b[l])

        qt, kt, vt, kf, vf, lf, conv, cst = _prompt_in(xp, *lw_in, *lw_conv, tri, *aux, t_real)
        attn = _prompt_attn(qt, kt, vt)
        xp = _out_ffn(xp, attn, conv, *lw_out, tm=OUT_TILE)
        kp.append(kf); vp.append(vf); lp.append(lf); cp.append(cst)

        qs, k_new, v_new, lf_new, conv_s, new_state = _sample_in(xs, *lw_in, state_t, l, *lw_conv)
        col = lambda t: t.reshape(n, N_HEADS, HEAD_DIM, 1)
        o = _sample_attn(page_table, col(qs), col(k_new), col(v_new), lf_new[:, :, None], us,
                         cache_kt, cache_vt, cache_xt, l)
        xs = _out_ffn(xs[None], o.reshape(1, n, a).astype(BF16), conv_s[None], *lw_out, tm=n)[0]
        ks.append(k_new); vs.append(v_new); ls.append(lf_new); cs.append(new_state)

    hd = (N_HEADS, HEAD_DIM)
    return (xp[:, N_META:t_real], xs[:, None, :],
            jnp.stack(kp).reshape(depth, b, t_real, *hd), jnp.stack(vp).reshape(depth, b, t_real, *hd),
            jnp.stack(lp), jnp.stack(cp),
            jnp.stack(ks).reshape(depth, n, 1, *hd), jnp.stack(vs).reshape(depth, n, 1, *hd),
            jnp.stack(ls)[:, :, None, :], jnp.transpose(jnp.stack(cs), (0, 2, 1, 3)))
```

```python
import functools

import numpy as np
import jax
import jax.numpy as jnp
from jax import lax
from jax.experimental import pallas as pl
from jax.experimental.pallas import tpu as pltpu

N_HEADS = 8
HEAD_DIM = 64
ATTN_WIDTH = N_HEADS * HEAD_DIM
N_META = 16
EPS = 1e-6
NEG_INF = -1e30

LANES = 128
SUBLANES = 8
HEAD_TILES = N_HEADS * LANES
ATTN_TILE = 768
ROW_TILE = 384
OUT_TILE = 768
CONV_CHUNK = 64
HIST_ROWS = 32
FF_CHUNK = 256
PAGES_PER_STEP = 8
VMEM_LIMIT = 56 * 1024 * 1024

F32 = jnp.float32
BF16 = jnp.bfloat16


def _aux_base(head):
    return HEAD_DIM if head % 2 == 0 else 0


def _resident(shape):
    zeros = (0,) * len(shape)
    return pl.BlockSpec(shape, lambda *_: zeros, pipeline_mode=pl.Buffered(1))


def _layer_block(arr, layer):
    idx = (layer,) + (0,) * (arr.ndim - 1)
    return pl.BlockSpec((None,) + arr.shape[1:], lambda *_: idx, pipeline_mode=pl.Buffered(1))


def _rmsnorm(x, g):
    return x * lax.rsqrt(jnp.mean(x * x, axis=-1, keepdims=True) + EPS) * g


def _log_sigmoid(x):
    return jnp.minimum(x, 0.0) - jnp.log1p(jnp.exp(-jnp.abs(x)))


def _sigmoid(x):
    return 1.0 / (1.0 + jnp.exp(-x))


def _split3(x):
    hi = x.astype(BF16)
    r1 = x - hi.astype(F32)
    mid = r1.astype(BF16)
    lo = (r1 - mid.astype(F32)).astype(BF16)
    return hi, mid, lo


def _conv_ln_silu(acc, g_ln, b_ln):
    mu = jnp.mean(acc, axis=-1, keepdims=True)
    xc = acc - mu
    y = xc * lax.rsqrt(jnp.mean(xc * xc, axis=-1, keepdims=True) + EPS) * g_ln + b_ln
    return y * _sigmoid(y)


def _prompt_in_kernel(*refs, n_alias, tm, t_real, conv_k):
    (x_ref, g_ref, w_ref, bf_ref, wdw_ref, bdw_ref, gln_ref, bln_ref, tri_ref,
     eq_ref, ek_ref, cq_ref, ck_ref, cv_ref,
     q_ref, k_ref, v_ref, kt_ref, vt_ref, lt_ref, conv_ref, cst_ref,
     ext_ref, sh_ref, carry_ref) = refs[n_alias:]
    i = pl.program_id(1)
    a = ATTN_WIDTH
    cw = conv_ref.shape[1]

    @pl.when(i == 0)
    def _():
        ext_ref[0:HIST_ROWS, :] = jnp.zeros((HIST_ROWS, cw), F32)
        carry_ref[...] = jnp.zeros_like(carry_ref)

    h = _rmsnorm(x_ref[...], g_ref[...]).astype(BF16)
    u = jnp.dot(h, w_ref[...], preferred_element_type=F32)
    uq = u[:, 0:a]
    uk = u[:, a:2 * a]
    uv = u[:, 2 * a:3 * a]
    kt_ref[...] = uk.T
    vt_ref[...] = uv.T

    fg0 = 3 * a + 2 * cw
    logf = _log_sigmoid(u[:, fg0:fg0 + LANES] + bf_ref[...])
    lt_ref[...] = logf.T[0:N_HEADS, :]
    tri = tri_ref[...]
    hi, mid, lo = _split3(logf)
    f = (jnp.dot(tri, hi, preferred_element_type=F32)
         + jnp.dot(tri, mid, preferred_element_type=F32)
         + jnp.dot(tri, lo, preferred_element_type=F32)) + carry_ref[...]
    carry_ref[...] = f[tm - 1:tm, :]
    fparts = jnp.concatenate(_split3(f), axis=1)
    aux_q = jnp.dot(fparts, eq_ref[...], preferred_element_type=F32) + cq_ref[...]
    aux_k = jnp.dot(fparts, ek_ref[...], preferred_element_type=F32) + ck_ref[...]
    lower = lax.broadcasted_iota(jnp.int32, (tm, LANES), 1) < HEAD_DIM
    for hd in range(N_HEADS):
        keep = lower if hd % 2 == 0 else jnp.logical_not(lower)
        pair = slice((hd // 2) * LANES, (hd // 2 + 1) * LANES)
        tile = slice(hd * LANES, (hd + 1) * LANES)
        q_ref[hd] = (jnp.where(keep, uq[:, pair], 0.0) + aux_q[:, tile]).astype(BF16)
        k_ref[hd] = (jnp.where(keep, uk[:, pair], 0.0) + aux_k[:, tile]).astype(BF16)
        v_ref[hd] = (jnp.where(keep, uv[:, pair], 0.0) + cv_ref[:, tile]).T.astype(BF16)

    ca = u[:, 3 * a:3 * a + cw]
    cb = u[:, 3 * a + cw:3 * a + 2 * cw]
    ext_ref[HIST_ROWS:HIST_ROWS + tm, :] = ca * _sigmoid(cb)
    base = HIST_ROWS - (conv_k - 1)
    span = sh_ref.shape[1]
    for r in range(1, SUBLANES):
        sh_ref[r - 1] = ext_ref[r:r + span, :]
    bdw = bdw_ref[...]
    gln = gln_ref[...]
    bln = bln_ref[...]
    for c in range(tm // CONV_CHUNK):
        r0 = c * CONV_CHUNK
        acc = jnp.broadcast_to(bdw, (CONV_CHUNK, cw))
        for j in range(conv_k):
            r = (base + j) % SUBLANES
            a0 = r0 + base + j - r
            rows = ext_ref[a0:a0 + CONV_CHUNK, :] if r == 0 else sh_ref[r - 1, a0:a0 + CONV_CHUNK, :]
            acc = acc + wdw_ref[j:j + 1, :] * rows
        conv_ref[r0:r0 + CONV_CHUNK, :] = _conv_ln_silu(acc, gln, bln).astype(BF16)

    last = (t_real - 1) // tm

    @pl.when(i == last)
    def _():
        off = HIST_ROWS + (t_real - (conv_k - 1) - last * tm)
        cst_ref[...] = ext_ref[off:off + conv_k - 1, :]

    ext_ref[0:HIST_ROWS, :] = ext_ref[tm:tm + HIST_ROWS, :]


def _prompt_in(x, layer, depth, result_bufs, g, w, bfg, wdw, bdw, gln, bln, shared, t_real):
    b, tp, d = x.shape
    tm = ROW_TILE
    a = ATTN_WIDTH
    cw = wdw.shape[2]
    conv_k = wdw.shape[1]
    assert conv_k - 1 <= HIST_ROWS and t_real - ((t_real - 1) // tm) * tm >= conv_k - 1
    row = lambda width: pl.BlockSpec((None, tm, width), lambda bi, i: (bi, i, 0))
    tiles = pl.BlockSpec((None, N_HEADS, tm, LANES), lambda bi, i: (bi, 0, i, 0))
    tiles_t = pl.BlockSpec((None, N_HEADS, LANES, tm), lambda bi, i: (bi, 0, 0, i))
    stacked_t = lambda rows: pl.BlockSpec((None, None, rows, tm), lambda bi, i: (layer, bi, 0, i))
    n_alias = len(result_bufs)
    out_shape = (
        jax.ShapeDtypeStruct((b, N_HEADS, tp, LANES), BF16),
        jax.ShapeDtypeStruct((b, N_HEADS, tp, LANES), BF16),
        jax.ShapeDtypeStruct((b, N_HEADS, LANES, tp), BF16),
        jax.ShapeDtypeStruct((depth, b, a, t_real), F32),
        jax.ShapeDtypeStruct((depth, b, a, t_real), F32),
        jax.ShapeDtypeStruct((depth, b, N_HEADS, t_real), F32),
        jax.ShapeDtypeStruct((b, tp, cw), BF16),
        jax.ShapeDtypeStruct((b, conv_k - 1, cw), F32),
    )
    out_specs = (tiles, tiles, tiles_t, stacked_t(a), stacked_t(a), stacked_t(N_HEADS), row(cw),
                 pl.BlockSpec((None, conv_k - 1, cw), lambda bi, i: (bi, 0, 0)))
    stacked = (g, w, bfg, wdw, bdw, gln, bln)
    in_specs = ([pl.BlockSpec(memory_space=pl.ANY)] * n_alias + [row(d)]
                + [_layer_block(p, layer) for p in stacked] + [_resident(c.shape) for c in shared])
    return pl.pallas_call(
        functools.partial(_prompt_in_kernel, n_alias=n_alias, tm=tm, t_real=t_real, conv_k=conv_k),
        grid=(b, tp // tm),
        in_specs=in_specs,
        out_specs=out_specs,
        out_shape=out_shape,
        input_output_aliases={k: 3 + k for k in range(n_alias)},
        scratch_shapes=[pltpu.VMEM((tm + HIST_ROWS, cw), F32),
                        pltpu.VMEM((SUBLANES - 1, tm + HIST_ROWS - SUBLANES, cw), F32),
                        pltpu.VMEM((1, LANES), F32)],
        compiler_params=pltpu.CompilerParams(
            dimension_semantics=("arbitrary", "arbitrary"), vmem_limit_bytes=VMEM_LIMIT),
        name="prompt_in",
    )(*result_bufs, x, *stacked, *shared)


def _prompt_attn_kernel(q_ref, k_ref, v_ref, o_ref, sa_ref, sb_ref, m_ref, acc_ref, *, tq):
    i = pl.program_id(2)
    m_ref[...] = jnp.full(m_ref.shape, NEG_INF, F32)
    acc_ref[...] = jnp.zeros_like(acc_ref)

    def scores(j, s_ref):
        start = pl.multiple_of(j * tq, tq)
        for hh in range(2):
            s_ref[hh] = lax.dot_general(k_ref[hh, pl.ds(start, tq), :], q_ref[hh],
                                        (((1,), (1,)), ((), ())), preferred_element_type=F32)

    def consume(j, s_ref, masked):
        start = pl.multiple_of(j * tq, tq)
        for hh in range(2):
            s = s_ref[hh]
            if masked:
                key = lax.broadcasted_iota(jnp.int32, (tq, tq), 0)
                qry = lax.broadcasted_iota(jnp.int32, (tq, tq), 1)
                s = jnp.where(key <= qry, s, NEG_INF)
            m_prev = m_ref[hh]
            m_new = jnp.maximum(m_prev, jnp.max(s, axis=0, keepdims=True))
            alpha = jnp.exp(m_prev - m_new)
            p = jnp.exp(s - m_new)
            acc_ref[hh] = alpha * acc_ref[hh] + jnp.dot(
                v_ref[hh, :, pl.ds(start, tq)], p.astype(BF16), preferred_element_type=F32)
            m_ref[hh] = m_new

    scores(0, sa_ref)

    def body(t, carry):
        scores(2 * t + 1, sb_ref)
        consume(2 * t, sa_ref, False)
        scores(2 * t + 2, sa_ref)
        consume(2 * t + 1, sb_ref, False)
        return carry

    lax.fori_loop(0, i // 2, body, 0)

    @pl.when(i % 2 == 0)
    def _():
        consume(i, sa_ref, True)

    @pl.when(i % 2 == 1)
    def _():
        scores(i, sb_ref)
        consume(i - 1, sa_ref, False)
        consume(i, sb_ref, True)

    outs = []
    for hh in range(2):
        acc = acc_ref[hh]
        ab = _aux_base(hh)
        outs.append((acc / acc[ab:ab + 1, :]).T)
    lane = lax.broadcasted_iota(jnp.int32, (tq, LANES), 1)
    o_ref[...] = jnp.where(lane < HEAD_DIM, outs[0], outs[1]).astype(o_ref.dtype)


def _prompt_attn(q, k, vt):
    b, nh, tp, _ = q.shape
    tq = ATTN_TILE
    qspec = pl.BlockSpec((None, 2, tq, LANES), lambda bi, hp, i: (bi, hp, i, 0))
    kspec = pl.BlockSpec((None, 2, tp, LANES), lambda bi, hp, i: (bi, hp, 0, 0))
    vspec = pl.BlockSpec((None, 2, LANES, tp), lambda bi, hp, i: (bi, hp, 0, 0))
    return pl.pallas_call(
        functools.partial(_prompt_attn_kernel, tq=tq),
        grid=(b, nh // 2, tp // tq),
        in_specs=[qspec, kspec, vspec],
        out_specs=pl.BlockSpec((None, tq, LANES), lambda bi, hp, i: (bi, i, hp)),
        out_shape=jax.ShapeDtypeStruct((b, tp, ATTN_WIDTH), BF16),
        scratch_shapes=[pltpu.VMEM((2, tq, tq), F32), pltpu.VMEM((2, tq, tq), F32),
                        pltpu.VMEM((2, 1, tq), F32), pltpu.VMEM((2, LANES, tq), F32)],
        compiler_params=pltpu.CompilerParams(
            dimension_semantics=("arbitrary", "arbitrary", "arbitrary"), vmem_limit_bytes=VMEM_LIMIT),
        name="prompt_attn",
    )(q, k, vt)


def _out_ffn_kernel(x_ref, attn_ref, conv_ref, woa_ref, woc_ref, gpm_ref, gpf_ref, gqf_ref,
                    wg_ref, wu_ref, wd_ref, o_ref, *, d_ff):
    mixed = (jnp.dot(attn_ref[...], woa_ref[...], preferred_element_type=F32)
             + jnp.dot(conv_ref[...], woc_ref[...], preferred_element_type=F32))
    x1 = x_ref[...] + _rmsnorm(mixed, gpm_ref[...])
    h = _rmsnorm(x1, gpf_ref[...]).astype(BF16)
    ff = jnp.zeros(x1.shape, F32)
    for c in range(d_ff // FF_CHUNK):
        c0 = c * FF_CHUNK
        gate = jnp.dot(h, wg_ref[:, c0:c0 + FF_CHUNK], preferred_element_type=F32)
        up = jnp.dot(h, wu_ref[:, c0:c0 + FF_CHUNK], preferred_element_type=F32)
        act = (gate * _sigmoid(gate) * up).astype(BF16)
        ff = ff + jnp.dot(act, wd_ref[c0:c0 + FF_CHUNK, :], preferred_element_type=F32)
    o_ref[...] = x1 + _rmsnorm(ff, gqf_ref[...])


def _out_ffn(x, attn, conv, layer, woa, woc, gpm, gpf, gqf, wg, wu, wd, tm):
    b, tp, d = x.shape
    d_ff = wg.shape[2]
    row = lambda width: pl.BlockSpec((None, tm, width), lambda bi, i: (bi, i, 0))
    stacked = (woa, woc, gpm, gpf, gqf, wg, wu, wd)
    return pl.pallas_call(
        functools.partial(_out_ffn_kernel, d_ff=d_ff),
        grid=(b, tp // tm),
        in_specs=[row(d), row(attn.shape[2]), row(conv.shape[2])] + [_layer_block(p, layer) for p in stacked],
        out_specs=row(d),
        out_shape=jax.ShapeDtypeStruct((b, tp, d), F32),
        compiler_params=pltpu.CompilerParams(
            dimension_semantics=("arbitrary", "arbitrary"), vmem_limit_bytes=VMEM_LIMIT),
        name="out_ffn",
    )(x, attn, conv, *stacked)


def _sample_in_kernel(x_ref, g_ref, w_ref, bf_ref, st_ref, wdw_ref, bdw_ref, gln_ref, bln_ref,
                      q_ref, k_ref, v_ref, lf_ref, conv_ref, ns_ref, *, conv_k):
    a = ATTN_WIDTH
    cw = conv_ref.shape[1]
    h = _rmsnorm(x_ref[...], g_ref[...]).astype(BF16)
    u = jnp.dot(h, w_ref[...], preferred_element_type=F32)
    q_ref[...] = u[:, 0:a]
    k_ref[...] = u[:, a:2 * a]
    v_ref[...] = u[:, 2 * a:3 * a]
    fg0 = 3 * a + 2 * cw
    lf_ref[...] = _log_sigmoid(u[:, fg0:fg0 + LANES] + bf_ref[...])[:, 0:N_HEADS]
    glu = u[:, 3 * a:3 * a + cw] * _sigmoid(u[:, 3 * a + cw:3 * a + 2 * cw])
    acc = bdw_ref[...] + wdw_ref[conv_k - 1:conv_k, :] * glu
    for j in range(conv_k - 1):
        acc = acc + wdw_ref[j:j + 1, :] * st_ref[j]
    conv_ref[...] = _conv_ln_silu(acc, gln_ref[...], bln_ref[...]).astype(BF16)
    for j in range(conv_k - 2):
        ns_ref[j] = st_ref[j + 1]
    ns_ref[conv_k - 2] = glu


def _sample_in(x, layer, g, w, bfg, state_t, wdw, bdw, gln, bln):
    n, d = x.shape
    a = ATTN_WIDTH
    cw = wdw.shape[2]
    conv_k = wdw.shape[1]
    out = lambda shape, dt: (jax.ShapeDtypeStruct(shape, dt), pl.BlockSpec(shape, lambda i: (0,) * len(shape)))
    outs = [out((n, a), F32), out((n, a), F32), out((n, a), F32), out((n, N_HEADS), F32),
            out((n, cw), BF16), out(state_t.shape[1:], F32)]
    stacked = (g, w, bfg, state_t, wdw, bdw, gln, bln)
    return pl.pallas_call(
        functools.partial(_sample_in_kernel, conv_k=conv_k),
        grid=(1,),
        in_specs=[pl.BlockSpec(x.shape, lambda i: (0, 0))] + [_layer_block(p, layer) for p in stacked],
        out_specs=tuple(o[1] for o in outs),
        out_shape=tuple(o[0] for o in outs),
        compiler_params=pltpu.CompilerParams(
            dimension_semantics=("arbitrary",), vmem_limit_bytes=VMEM_LIMIT),
        name="sample_in",
    )(x, *stacked)


def _sample_attn_kernel(pt_ref, q_ref, kn_ref, vn_ref, lfn_ref, us_ref, *rest, pps):
    k_refs = rest[0:pps]
    v_refs = rest[pps:2 * pps]
    x_refs = rest[2 * pps:3 * pps]
    o_ref = rest[3 * pps]
    qb_ref, m_ref, l_ref, acc_ref, carry_ref = rest[3 * pps + 1:]
    g = pl.program_id(1)
    page = qb_ref.shape[2]

    @pl.when(g == 0)
    def _():
        for hd in range(N_HEADS):
            qb_ref[hd] = jnp.broadcast_to(q_ref[hd], (HEAD_DIM, page))
        m_ref[...] = jnp.full(m_ref.shape, NEG_INF, F32)
        l_ref[...] = jnp.zeros_like(l_ref)
        acc_ref[...] = jnp.zeros_like(acc_ref)
        carry_ref[...] = jnp.zeros_like(carry_ref)

    us = us_ref[...]
    logf = jnp.concatenate([x_refs[idx][...] for idx in range(pps)], axis=0)
    hi, mid, lo = _split3(logf)
    yr = (jnp.dot(hi, us, preferred_element_type=F32)
          + jnp.dot(mid, us, preferred_element_type=F32)
          + jnp.dot(lo, us, preferred_element_type=F32))
    carry = carry_ref[...]
    lfn = lfn_ref[...]
    scores = []
    for idx in range(pps):
        rows = slice(idx * N_HEADS, (idx + 1) * N_HEADS)
        qk = jnp.concatenate(
            [jnp.sum(qb_ref[hd] * k_refs[idx][hd], axis=0, keepdims=True) for hd in range(N_HEADS)], axis=0)
        scores.append(qk + (yr[rows, 0:page] + carry + lfn))
        carry = carry + yr[rows, page:2 * page]
    carry_ref[...] = carry

    m_prev = m_ref[...]
    m_new = m_prev
    for s in scores:
        m_new = jnp.maximum(m_new, s)
    alpha = jnp.exp(m_prev - m_new)
    probs = [jnp.exp(s - m_new) for s in scores]
    l_new = alpha * l_ref[...]
    for p in probs:
        l_new = l_new + p
    l_ref[...] = l_new
    m_ref[...] = m_new
    for hd in range(N_HEADS):
        acc = alpha[hd:hd + 1, :] * acc_ref[hd]
        for idx in range(pps):
            acc = acc + probs[idx][hd:hd + 1, :] * v_refs[idx][hd]
        acc_ref[hd] = acc

    @pl.when(g == pl.num_programs(1) - 1)
    def _():
        for hd in range(N_HEADS):
            m = m_ref[hd:hd + 1, :]
            m_all = jnp.max(m, axis=1, keepdims=True)
            w = jnp.exp(m - m_all)
            l_all = jnp.sum(l_ref[hd:hd + 1, :] * w, axis=1, keepdims=True)
            o_all = jnp.sum(acc_ref[hd] * w, axis=1, keepdims=True)
            s_new = jnp.sum(q_ref[hd] * kn_ref[hd], axis=0, keepdims=True)
            m_fin = jnp.maximum(m_all, s_new)
            a_old = jnp.exp(m_all - m_fin)
            p_new = jnp.exp(s_new - m_fin)
            o_ref[hd] = (a_old * o_all + p_new * vn_ref[hd]) / (a_old * l_all + p_new)


def _sample_attn(page_table, q, kn, vn, lfn, us, cache_kt, cache_vt, cache_xt, layer):
    n, n_pages = page_table.shape
    page = cache_kt.shape[4]
    pps = PAGES_PER_STEP
    steps = n_pages // pps
    per_seq = lambda arr: pl.BlockSpec((None,) + arr.shape[1:], lambda bi, g, pt: (bi,) + (0,) * (arr.ndim - 1))

    def paged(arr, idx):
        zeros = (0,) * (arr.ndim - 2)
        return pl.BlockSpec(
            (None, None) + arr.shape[2:],
            lambda bi, g, pt: (layer, pt[bi, n_pages - 1 - (g * pps + idx)]) + zeros)

    in_specs = [per_seq(q), per_seq(kn), per_seq(vn), per_seq(lfn),
                pl.BlockSpec(us.shape, lambda bi, g, pt: (0, 0))]
    in_specs += [paged(cache_kt, idx) for idx in range(pps)]
    in_specs += [paged(cache_vt, idx) for idx in range(pps)]
    in_specs += [paged(cache_xt, idx) for idx in range(pps)]
    grid_spec = pltpu.PrefetchScalarGridSpec(
        num_scalar_prefetch=1,
        grid=(n, steps),
        in_specs=in_specs,
        out_specs=per_seq(q),
        scratch_shapes=[
            pltpu.VMEM((N_HEADS, HEAD_DIM, page), F32), pltpu.VMEM((N_HEADS, page), F32),
            pltpu.VMEM((N_HEADS, page), F32), pltpu.VMEM((N_HEADS, HEAD_DIM, page), F32),
            pltpu.VMEM((N_HEADS, page), F32)],
    )
    args = [page_table, q, kn, vn, lfn, us] + [cache_kt] * pps + [cache_vt] * pps + [cache_xt] * pps
    return pl.pallas_call(
        functools.partial(_sample_attn_kernel, pps=pps),
        grid_spec=grid_spec,
        out_shape=jax.ShapeDtypeStruct(q.shape, F32),
        compiler_params=pltpu.CompilerParams(
            dimension_semantics=("arbitrary", "arbitrary"), vmem_limit_bytes=VMEM_LIMIT),
        name="sample_attn",
    )(*args)


def _suffix_matrix(page):
    src = np.arange(page)[:, None]
    dst = np.arange(page)[None, :]
    after = src > dst
    return jnp.asarray(np.concatenate([after, np.ones_like(after)], axis=1).astype(np.float32), BF16)


def _aux_constants():
    eq = np.zeros((3 * LANES, HEAD_TILES), np.float32)
    ek = np.zeros((3 * LANES, HEAD_TILES), np.float32)
    cq = np.zeros((1, HEAD_TILES), np.float32)
    ck = np.zeros((1, HEAD_TILES), np.float32)
    cv = np.zeros((1, HEAD_TILES), np.float32)
    for hd in range(N_HEADS):
        base = hd * LANES + _aux_base(hd)
        for part in range(3):
            eq[part * LANES + hd, base + part] = 1.0
            ek[part * LANES + hd, base + 3 + part] = -1.0
            cq[0, base + 3 + part] = 1.0
            ck[0, base + part] = 1.0
        cv[0, base] = 1.0
    return (jnp.asarray(eq, BF16), jnp.asarray(ek, BF16), jnp.asarray(cq), jnp.asarray(ck), jnp.asarray(cv))


def kernel(x_prompt, x_sample, cache_k, cache_v, cache_logf, state_conv, page_table, meta_tokens,
           w_in, b_forget, w_dw, b_dw, g_conv_ln, b_conv_ln, w_out, g_pre_mix, g_post_mix,
           g_pre_ffn, g_post_ffn, w_gate, w_up, w_down):
    depth = w_in.shape[0]
    b, seq, d = x_prompt.shape
    n = x_sample.shape[0]
    a = ATTN_WIDTH
    cw = w_dw.shape[2]
    page = cache_k.shape[2]
    t_real = N_META + seq
    tp = -(-t_real // ATTN_TILE) * ATTN_TILE

    scale = HEAD_DIM ** -0.5
    fg0 = 3 * a
    ca0 = fg0 + N_HEADS
    w_in_r = jnp.concatenate([
        w_in[:, :, 0:a] * scale, w_in[:, :, a:3 * a], w_in[:, :, ca0:ca0 + 2 * cw],
        jnp.pad(w_in[:, :, fg0:ca0], ((0, 0), (0, 0), (0, LANES - N_HEADS)))], axis=2).astype(BF16)
    vec = lambda p: p[:, None, :]
    bfg = vec(jnp.pad(b_forget, ((0, 0), (0, LANES - N_HEADS))))
    in_params = (vec(g_pre_mix), w_in_r, bfg)
    conv_params = (w_dw, vec(b_dw), vec(g_conv_ln), vec(b_conv_ln))
    out_params = (w_out[:, 0:a, :].astype(BF16), w_out[:, a:, :].astype(BF16), vec(g_post_mix),
                  vec(g_pre_ffn), vec(g_post_ffn), w_gate.astype(BF16), w_up.astype(BF16),
                  w_down.astype(BF16))
    tri = jnp.asarray(np.tril(np.ones((ROW_TILE, ROW_TILE), np.float32)), BF16)
    shared = (tri,) + _aux_constants()
    us = _suffix_matrix(page)
    cache_kt = jnp.transpose(cache_k, (0, 1, 3, 4, 2))
    cache_vt = jnp.transpose(cache_v, (0, 1, 3, 4, 2))
    cache_xt = jnp.transpose(cache_logf, (0, 1, 3, 2))
    state_t = jnp.transpose(state_conv, (0, 2, 1, 3))

    meta = jnp.broadcast_to(meta_tokens[None].astype(x_prompt.dtype), (b, N_META, d))
    xp = jnp.concatenate([meta, x_prompt, jnp.zeros((b, tp - t_real, d), x_prompt.dtype)], axis=1)
    xs = x_sample.reshape(n, d)

    result_bufs = ()
    cp, ks, vs, ls, cs = ([] for _ in range(5))
    for l in range(depth):
        qt, kt, vt, kbuf, vbuf, lbuf, conv, cst = _prompt_in(
            xp, l, depth, result_bufs, *in_params, *conv_params, shared, t_real)
        result_bufs = (kbuf, vbuf, lbuf)
        attn = _prompt_attn(qt, kt, vt)
        xp = _out_ffn(xp, attn, conv, l, *out_params, tm=OUT_TILE)
        cp.append(cst)

        qs, k_new, v_new, lf_new, conv_s, new_state = _sample_in(xs, l, *in_params, state_t, *conv_params)
        col = lambda t: t.reshape(n, N_HEADS, HEAD_DIM, 1)
        o = _sample_attn(page_table, col(qs), col(k_new), col(v_new), lf_new[:, :, None], us,
                         cache_kt, cache_vt, cache_xt, l)
        xs = _out_ffn(xs[None], o.reshape(1, n, a).astype(BF16), conv_s[None], l, *out_params, tm=n)[0]
        ks.append(k_new); vs.append(v_new); ls.append(lf_new); cs.append(new_state)

    kbuf, vbuf, lbuf = result_bufs
    hd = (N_HEADS, HEAD_DIM)
    per_head = lambda buf: jnp.transpose(buf.reshape(depth, b, *hd, t_real), (0, 1, 4, 2, 3))
    return (xp[:, N_META:t_real], xs[:, None, :],
            per_head(kbuf), per_head(vbuf), jnp.transpose(lbuf, (0, 1, 3, 2)), jnp.stack(cp),
            jnp.stack(ks).reshape(depth, n, 1, *hd), jnp.stack(vs).reshape(depth, n, 1, *hd),
            jnp.stack(ls)[:, :, None, :], jnp.transpose(jnp.stack(cs), (0, 2, 1, 3)))
```

```python
import functools

import numpy as np
import jax
import jax.numpy as jnp
from jax import lax
from jax.experimental import pallas as pl
from jax.experimental.pallas import tpu as pltpu

N_HEADS = 8
HEAD_DIM = 64
ATTN_WIDTH = N_HEADS * HEAD_DIM
N_META = 16
EPS = 1e-6
NEG_INF = -1e30

LANES = 128
SUBLANES = 8
HEAD_TILES = N_HEADS * LANES
ATTN_TILE = 768
DIAG_BLOCK = 256
ROW_TILE = 384
OUT_TILE = 768
CONV_CHUNK = 64
HIST_ROWS = 32
FF_CHUNK = 256
PAGES_PER_STEP = 16
VMEM_LIMIT = 56 * 1024 * 1024

F32 = jnp.float32
BF16 = jnp.bfloat16


def _aux_base(head):
    return HEAD_DIM if head % 2 == 0 else 0


def _resident(shape):
    zeros = (0,) * len(shape)
    return pl.BlockSpec(shape, lambda *_: zeros, pipeline_mode=pl.Buffered(1))


def _layer_block(arr, layer):
    idx = (layer,) + (0,) * (arr.ndim - 1)
    return pl.BlockSpec((None,) + arr.shape[1:], lambda *_: idx, pipeline_mode=pl.Buffered(1))


def _rmsnorm(x, g):
    return x * lax.rsqrt(jnp.mean(x * x, axis=-1, keepdims=True) + EPS) * g


def _log_sigmoid(x):
    return jnp.minimum(x, 0.0) - jnp.log1p(jnp.exp(-jnp.abs(x)))


def _sigmoid(x):
    return 1.0 / (1.0 + jnp.exp(-x))


def _split3(x):
    hi = x.astype(BF16)
    r1 = x - hi.astype(F32)
    mid = r1.astype(BF16)
    lo = (r1 - mid.astype(F32)).astype(BF16)
    return hi, mid, lo


def _conv_ln_silu(acc, g_ln, b_ln):
    mu = jnp.mean(acc, axis=-1, keepdims=True)
    xc = acc - mu
    y = xc * lax.rsqrt(jnp.mean(xc * xc, axis=-1, keepdims=True) + EPS) * g_ln + b_ln
    return y * _sigmoid(y)


def _prompt_in_kernel(*refs, n_alias, tm, t_real, conv_k):
    (x_ref, g_ref, w_ref, bf_ref, wdw_ref, bdw_ref, gln_ref, bln_ref, tri_ref,
     eq_ref, ek_ref, cq_ref, ck_ref, cv_ref,
     q_ref, k_ref, v_ref, kt_ref, vt_ref, lt_ref, conv_ref, cst_ref,
     ext_ref, sh_ref, carry_ref) = refs[n_alias:]
    i = pl.program_id(1)
    a = ATTN_WIDTH
    cw = conv_ref.shape[1]

    @pl.when(i == 0)
    def _():
        ext_ref[0:HIST_ROWS, :] = jnp.zeros((HIST_ROWS, cw), F32)
        carry_ref[...] = jnp.zeros_like(carry_ref)

    h = _rmsnorm(x_ref[...], g_ref[...]).astype(BF16)
    u = jnp.dot(h, w_ref[...], preferred_element_type=F32)
    uq = u[:, 0:a]
    uk = u[:, a:2 * a]
    uv = u[:, 2 * a:3 * a]
    kt_ref[...] = uk.T
    vt_ref[...] = uv.T

    fg0 = 3 * a + 2 * cw
    logf = _log_sigmoid(u[:, fg0:fg0 + LANES] + bf_ref[...])
    lt_ref[...] = logf.T[0:N_HEADS, :]
    tri = tri_ref[...]
    hi, mid, lo = _split3(logf)
    f = (jnp.dot(tri, hi, preferred_element_type=F32)
         + jnp.dot(tri, mid, preferred_element_type=F32)
         + jnp.dot(tri, lo, preferred_element_type=F32)) + carry_ref[...]
    carry_ref[...] = f[tm - 1:tm, :]
    fparts = jnp.concatenate(_split3(f), axis=1)
    aux_q = jnp.dot(fparts, eq_ref[...], preferred_element_type=F32) + cq_ref[...]
    aux_k = jnp.dot(fparts, ek_ref[...], preferred_element_type=F32) + ck_ref[...]
    lower = lax.broadcasted_iota(jnp.int32, (tm, LANES), 1) < HEAD_DIM
    for hd in range(N_HEADS):
        keep = lower if hd % 2 == 0 else jnp.logical_not(lower)
        pair = slice((hd // 2) * LANES, (hd // 2 + 1) * LANES)
        tile = slice(hd * LANES, (hd + 1) * LANES)
        q_ref[hd] = jnp.where(keep, uq[:, pair], aux_q[:, tile]).astype(BF16)
        k_ref[hd] = jnp.where(keep, uk[:, pair], aux_k[:, tile]).astype(BF16)
        v_ref[hd] = jnp.where(keep, uv[:, pair], cv_ref[:, tile]).T.astype(BF16)

    ca = u[:, 3 * a:3 * a + cw]
    cb = u[:, 3 * a + cw:3 * a + 2 * cw]
    ext_ref[HIST_ROWS:HIST_ROWS + tm, :] = ca * _sigmoid(cb)
    base = HIST_ROWS - (conv_k - 1)
    span = sh_ref.shape[1]
    for r in range(1, SUBLANES):
        sh_ref[r - 1] = ext_ref[r:r + span, :]
    bdw = bdw_ref[...]
    gln = gln_ref[...]
    bln = bln_ref[...]
    for c in range(tm // CONV_CHUNK):
        r0 = c * CONV_CHUNK
        acc = jnp.broadcast_to(bdw, (CONV_CHUNK, cw))
        for j in range(conv_k):
            r = (base + j) % SUBLANES
            a0 = r0 + base + j - r
            rows = ext_ref[a0:a0 + CONV_CHUNK, :] if r == 0 else sh_ref[r - 1, a0:a0 + CONV_CHUNK, :]
            acc = acc + wdw_ref[j:j + 1, :] * rows
        conv_ref[r0:r0 + CONV_CHUNK, :] = _conv_ln_silu(acc, gln, bln).astype(BF16)

    last = (t_real - 1) // tm

    @pl.when(i == last)
    def _():
        off = HIST_ROWS + (t_real - (conv_k - 1) - last * tm)
        cst_ref[...] = ext_ref[off:off + conv_k - 1, :]

    ext_ref[0:HIST_ROWS, :] = ext_ref[tm:tm + HIST_ROWS, :]


def _prompt_in(x, layer, depth, result_bufs, g, w, bfg, wdw, bdw, gln, bln, shared, t_real):
    b, tp, d = x.shape
    tm = ROW_TILE
    a = ATTN_WIDTH
    cw = wdw.shape[2]
    conv_k = wdw.shape[1]
    assert conv_k - 1 <= HIST_ROWS and t_real - ((t_real - 1) // tm) * tm >= conv_k - 1
    row = lambda width: pl.BlockSpec((None, tm, width), lambda bi, i: (bi, i, 0))
    tiles = pl.BlockSpec((None, N_HEADS, tm, LANES), lambda bi, i: (bi, 0, i, 0))
    tiles_t = pl.BlockSpec((None, N_HEADS, LANES, tm), lambda bi, i: (bi, 0, 0, i))
    stacked_t = lambda rows: pl.BlockSpec((None, None, rows, tm), lambda bi, i: (layer, bi, 0, i))
    n_alias = len(result_bufs)
    out_shape = (
        jax.ShapeDtypeStruct((b, N_HEADS, tp, LANES), BF16),
        jax.ShapeDtypeStruct((b, N_HEADS, tp, LANES), BF16),
        jax.ShapeDtypeStruct((b, N_HEADS, LANES, tp), BF16),
        jax.ShapeDtypeStruct((depth, b, a, t_real), F32),
        jax.ShapeDtypeStruct((depth, b, a, t_real), F32),
        jax.ShapeDtypeStruct((depth, b, N_HEADS, t_real), F32),
        jax.ShapeDtypeStruct((b, tp, cw), BF16),
        jax.ShapeDtypeStruct((b, conv_k - 1, cw), F32),
    )
    out_specs = (tiles, tiles, tiles_t, stacked_t(a), stacked_t(a), stacked_t(N_HEADS), row(cw),
                 pl.BlockSpec((None, conv_k - 1, cw), lambda bi, i: (bi, 0, 0)))
    stacked = (g, w, bfg, wdw, bdw, gln, bln)
    in_specs = ([pl.BlockSpec(memory_space=pl.ANY)] * n_alias + [row(d)]
                + [_layer_block(p, layer) for p in stacked] + [_resident(c.shape) for c in shared])
    return pl.pallas_call(
        functools.partial(_prompt_in_kernel, n_alias=n_alias, tm=tm, t_real=t_real, conv_k=conv_k),
        grid=(b, tp // tm),
        in_specs=in_specs,
        out_specs=out_specs,
        out_shape=out_shape,
        input_output_aliases={k: 3 + k for k in range(n_alias)},
        scratch_shapes=[pltpu.VMEM((tm + HIST_ROWS, cw), F32),
                        pltpu.VMEM((SUBLANES - 1, tm + HIST_ROWS - SUBLANES, cw), F32),
                        pltpu.VMEM((1, LANES), F32)],
        compiler_params=pltpu.CompilerParams(
            dimension_semantics=("arbitrary", "arbitrary"), vmem_limit_bytes=VMEM_LIMIT),
        name="prompt_in",
    )(*result_bufs, x, *stacked, *shared)


def _prompt_attn_kernel(q_ref, k_ref, v_ref, o_ref, sa_ref, sb_ref, m_ref, acc_ref, *, tq):
    i = pl.program_id(2)
    m_ref[...] = jnp.full(m_ref.shape, NEG_INF, F32)
    acc_ref[...] = jnp.zeros_like(acc_ref)

    def scores(j, s_ref):
        start = pl.multiple_of(j * tq, tq)
        for hh in range(2):
            s_ref[hh] = lax.dot_general(k_ref[hh, pl.ds(start, tq), :], q_ref[hh],
                                        (((1,), (1,)), ((), ())), preferred_element_type=F32)

    def consume(j, s_ref):
        start = pl.multiple_of(j * tq, tq)
        for hh in range(2):
            s = s_ref[hh]
            m_prev = m_ref[hh]
            m_new = jnp.maximum(m_prev, jnp.max(s, axis=0, keepdims=True))
            alpha = jnp.exp(m_prev - m_new)
            p = jnp.exp(s - m_new)
            acc_ref[hh] = alpha * acc_ref[hh] + jnp.dot(
                v_ref[hh, :, pl.ds(start, tq)], p.astype(BF16), preferred_element_type=F32)
            m_ref[hh] = m_new

    def consume_diag(s_ref):
        base = pl.multiple_of(i * tq, tq)
        for hh in range(2):
            for q0 in range(0, tq, DIAG_BLOCK):
                nk = q0 + DIAG_BLOCK
                s = s_ref[hh, 0:nk, q0:nk]
                key = lax.broadcasted_iota(jnp.int32, (nk, DIAG_BLOCK), 0)
                qry = lax.broadcasted_iota(jnp.int32, (nk, DIAG_BLOCK), 1) + q0
                s = jnp.where(key <= qry, s, NEG_INF)
                m_prev = m_ref[hh, :, q0:nk]
                m_new = jnp.maximum(m_prev, jnp.max(s, axis=0, keepdims=True))
                alpha = jnp.exp(m_prev - m_new)
                p = jnp.exp(s - m_new)
                acc_ref[hh, :, q0:nk] = alpha * acc_ref[hh, :, q0:nk] + jnp.dot(
                    v_ref[hh, :, pl.ds(base, nk)], p.astype(BF16), preferred_element_type=F32)
                m_ref[hh, :, q0:nk] = m_new

    scores(0, sa_ref)

    def body(t, carry):
        scores(2 * t + 1, sb_ref)
        consume(2 * t, sa_ref)
        scores(2 * t + 2, sa_ref)
        consume(2 * t + 1, sb_ref)
        return carry

    lax.fori_loop(0, i // 2, body, 0)

    @pl.when(i % 2 == 0)
    def _():
        consume_diag(sa_ref)

    @pl.when(i % 2 == 1)
    def _():
        scores(i, sb_ref)
        consume(i - 1, sa_ref)
        consume_diag(sb_ref)

    outs = []
    for hh in range(2):
        acc = acc_ref[hh]
        ab = _aux_base(hh)
        outs.append((acc / acc[ab:ab + 1, :]).T)
    lane = lax.broadcasted_iota(jnp.int32, (tq, LANES), 1)
    o_ref[...] = jnp.where(lane < HEAD_DIM, outs[0], outs[1]).astype(o_ref.dtype)


def _prompt_attn(q, k, vt):
    b, nh, tp, _ = q.shape
    tq = ATTN_TILE
    qspec = pl.BlockSpec((None, 2, tq, LANES), lambda bi, hp, i: (bi, hp, i, 0))
    kspec = pl.BlockSpec((None, 2, tp, LANES), lambda bi, hp, i: (bi, hp, 0, 0))
    vspec = pl.BlockSpec((None, 2, LANES, tp), lambda bi, hp, i: (bi, hp, 0, 0))
    return pl.pallas_call(
        functools.partial(_prompt_attn_kernel, tq=tq),
        grid=(b, nh // 2, tp // tq),
        in_specs=[qspec, kspec, vspec],
        out_specs=pl.BlockSpec((None, tq, LANES), lambda bi, hp, i: (bi, i, hp)),
        out_shape=jax.ShapeDtypeStruct((b, tp, ATTN_WIDTH), BF16),
        scratch_shapes=[pltpu.VMEM((2, tq, tq), F32), pltpu.VMEM((2, tq, tq), F32),
                        pltpu.VMEM((2, 1, tq), F32), pltpu.VMEM((2, LANES, tq), F32)],
        compiler_params=pltpu.CompilerParams(
            dimension_semantics=("arbitrary", "arbitrary", "arbitrary"), vmem_limit_bytes=VMEM_LIMIT),
        name="prompt_attn",
    )(q, k, vt)


def _out_ffn_kernel(x_ref, attn_ref, conv_ref, woa_ref, woc_ref, gpm_ref, gpf_ref, gqf_ref,
                    wg_ref, wu_ref, wd_ref, o_ref, *, d_ff):
    mixed = (jnp.dot(attn_ref[...], woa_ref[...], preferred_element_type=F32)
             + jnp.dot(conv_ref[...], woc_ref[...], preferred_element_type=F32))
    x1 = x_ref[...] + _rmsnorm(mixed, gpm_ref[...])
    h = _rmsnorm(x1, gpf_ref[...]).astype(BF16)
    ff = jnp.zeros(x1.shape, F32)
    for c in range(d_ff // FF_CHUNK):
        c0 = c * FF_CHUNK
        gate = jnp.dot(h, wg_ref[:, c0:c0 + FF_CHUNK], preferred_element_type=F32)
        up = jnp.dot(h, wu_ref[:, c0:c0 + FF_CHUNK], preferred_element_type=F32)
        act = (gate * _sigmoid(gate) * up).astype(BF16)
        ff = ff + jnp.dot(act, wd_ref[c0:c0 + FF_CHUNK, :], preferred_element_type=F32)
    o_ref[...] = x1 + _rmsnorm(ff, gqf_ref[...])


def _out_ffn(x, attn, conv, layer, woa, woc, gpm, gpf, gqf, wg, wu, wd, tm):
    b, tp, d = x.shape
    d_ff = wg.shape[2]
    row = lambda width: pl.BlockSpec((None, tm, width), lambda bi, i: (bi, i, 0))
    stacked = (woa, woc, gpm, gpf, gqf, wg, wu, wd)
    return pl.pallas_call(
        functools.partial(_out_ffn_kernel, d_ff=d_ff),
        grid=(b, tp // tm),
        in_specs=[row(d), row(attn.shape[2]), row(conv.shape[2])] + [_layer_block(p, layer) for p in stacked],
        out_specs=row(d),
        out_shape=jax.ShapeDtypeStruct((b, tp, d), F32),
        compiler_params=pltpu.CompilerParams(
            dimension_semantics=("arbitrary", "arbitrary"), vmem_limit_bytes=VMEM_LIMIT),
        name="out_ffn",
    )(x, attn, conv, *stacked)


def _sample_in_kernel(x_ref, g_ref, w_ref, bf_ref, st_ref, wdw_ref, bdw_ref, gln_ref, bln_ref,
                      q_ref, k_ref, v_ref, lf_ref, conv_ref, ns_ref, *, conv_k):
    a = ATTN_WIDTH
    cw = conv_ref.shape[1]
    h = _rmsnorm(x_ref[...], g_ref[...]).astype(BF16)
    u = jnp.dot(h, w_ref[...], preferred_element_type=F32)
    q_ref[...] = u[:, 0:a]
    k_ref[...] = u[:, a:2 * a]
    v_ref[...] = u[:, 2 * a:3 * a]
    fg0 = 3 * a + 2 * cw
    lf_ref[...] = _log_sigmoid(u[:, fg0:fg0 + LANES] + bf_ref[...])[:, 0:N_HEADS]
    glu = u[:, 3 * a:3 * a + cw] * _sigmoid(u[:, 3 * a + cw:3 * a + 2 * cw])
    acc = bdw_ref[...] + wdw_ref[conv_k - 1:conv_k, :] * glu
    for j in range(conv_k - 1):
        acc = acc + wdw_ref[j:j + 1, :] * st_ref[j]
    conv_ref[...] = _conv_ln_silu(acc, gln_ref[...], bln_ref[...]).astype(BF16)
    for j in range(conv_k - 2):
        ns_ref[j] = st_ref[j + 1]
    ns_ref[conv_k - 2] = glu


def _sample_in(x, layer, g, w, bfg, state_t, wdw, bdw, gln, bln):
    n, d = x.shape
    a = ATTN_WIDTH
    cw = wdw.shape[2]
    conv_k = wdw.shape[1]
    out = lambda shape, dt: (jax.ShapeDtypeStruct(shape, dt), pl.BlockSpec(shape, lambda i: (0,) * len(shape)))
    outs = [out((n, a), F32), out((n, a), F32), out((n, a), F32), out((n, N_HEADS), F32),
            out((n, cw), BF16), out(state_t.shape[1:], F32)]
    stacked = (g, w, bfg, state_t, wdw, bdw, gln, bln)
    return pl.pallas_call(
        functools.partial(_sample_in_kernel, conv_k=conv_k),
        grid=(1,),
        in_specs=[pl.BlockSpec(x.shape, lambda i: (0, 0))] + [_layer_block(p, layer) for p in stacked],
        out_specs=tuple(o[1] for o in outs),
        out_shape=tuple(o[0] for o in outs),
        compiler_params=pltpu.CompilerParams(
            dimension_semantics=("arbitrary",), vmem_limit_bytes=VMEM_LIMIT),
        name="sample_in",
    )(x, *stacked)


def _sample_attn_kernel(pt_ref, q_ref, kn_ref, vn_ref, lfn_ref, us_ref, *rest, pps):
    k_refs = rest[0:pps]
    v_refs = rest[pps:2 * pps]
    x_refs = rest[2 * pps:3 * pps]
    o_ref = rest[3 * pps]
    qb_ref, m_ref, l_ref, acc_ref, carry_ref = rest[3 * pps + 1:]
    g = pl.program_id(1)
    page = qb_ref.shape[2]

    eye = (lax.broadcasted_iota(jnp.int32, (HEAD_DIM, HEAD_DIM), 0)
           == lax.broadcasted_iota(jnp.int32, (HEAD_DIM, HEAD_DIM), 1))

    def head_column(row_ref, hd):
        row = row_ref[:, hd * HEAD_DIM:(hd + 1) * HEAD_DIM]
        return jnp.sum(jnp.where(eye, row, 0.0), axis=1, keepdims=True)

    @pl.when(g == 0)
    def _():
        for hd in range(N_HEADS):
            qb_ref[hd] = jnp.broadcast_to(head_column(q_ref, hd), (HEAD_DIM, page))
        m_ref[...] = jnp.full(m_ref.shape, NEG_INF, F32)
        l_ref[...] = jnp.zeros_like(l_ref)
        acc_ref[...] = jnp.zeros_like(acc_ref)
        carry_ref[...] = jnp.zeros_like(carry_ref)

    us = us_ref[...]
    logf = jnp.concatenate([x_refs[idx][...] for idx in range(pps)], axis=0)
    hi, mid, lo = _split3(logf)
    yr = (jnp.dot(hi, us, preferred_element_type=F32)
          + jnp.dot(mid, us, preferred_element_type=F32)
          + jnp.dot(lo, us, preferred_element_type=F32))
    carry = carry_ref[...]
    lfn = lfn_ref[...]
    scores = []
    for idx in range(pps):
        rows = slice(idx * N_HEADS, (idx + 1) * N_HEADS)
        qk = jnp.concatenate(
            [jnp.sum(qb_ref[hd] * k_refs[idx][hd], axis=0, keepdims=True) for hd in range(N_HEADS)], axis=0)
        scores.append(qk + (yr[rows, 0:page] + carry + lfn))
        carry = carry + yr[rows, page:2 * page]
    carry_ref[...] = carry

    m_prev = m_ref[...]
    m_new = m_prev
    for s in scores:
        m_new = jnp.maximum(m_new, s)
    alpha = jnp.exp(m_prev - m_new)
    probs = [jnp.exp(s - m_new) for s in scores]
    l_new = alpha * l_ref[...]
    for p in probs:
        l_new = l_new + p
    l_ref[...] = l_new
    m_ref[...] = m_new
    for hd in range(N_HEADS):
        acc = alpha[hd:hd + 1, :] * acc_ref[hd]
        for idx in range(pps):
            acc = acc + probs[idx][hd:hd + 1, :] * v_refs[idx][hd]
        acc_ref[hd] = acc

    @pl.when(g == pl.num_programs(1) - 1)
    def _():
        for hd in range(N_HEADS):
            m = m_ref[hd:hd + 1, :]
            m_all = jnp.max(m, axis=1, keepdims=True)
            w = jnp.exp(m - m_all)
            l_all = jnp.sum(l_ref[hd:hd + 1, :] * w, axis=1, keepdims=True)
            o_all = jnp.sum(acc_ref[hd] * w, axis=1, keepdims=True)
            cols = slice(hd * HEAD_DIM, (hd + 1) * HEAD_DIM)
            s_new = jnp.sum(q_ref[:, cols] * kn_ref[:, cols], axis=1, keepdims=True)
            m_fin = jnp.maximum(m_all, s_new)
            a_old = jnp.exp(m_all - m_fin)
            p_new = jnp.exp(s_new - m_fin)
            o_ref[hd] = (a_old * o_all + p_new * head_column(vn_ref, hd)) / (a_old * l_all + p_new)


def _sample_attn(page_table, q, kn, vn, lfn, us, cache_kt, cache_vt, cache_xt, layer):
    n, n_pages = page_table.shape
    page = cache_kt.shape[4]
    pps = PAGES_PER_STEP
    steps = n_pages // pps
    per_seq = lambda arr: pl.BlockSpec((None,) + arr.shape[1:], lambda bi, g, pt: (bi,) + (0,) * (arr.ndim - 1))

    def paged(arr, idx):
        zeros = (0,) * (arr.ndim - 2)
        return pl.BlockSpec(
            (None, None) + arr.shape[2:],
            lambda bi, g, pt: (layer, pt[bi, n_pages - 1 - (g * pps + idx)]) + zeros)

    in_specs = [per_seq(q), per_seq(kn), per_seq(vn), per_seq(lfn),
                pl.BlockSpec(us.shape, lambda bi, g, pt: (0, 0))]
    in_specs += [paged(cache_kt, idx) for idx in range(pps)]
    in_specs += [paged(cache_vt, idx) for idx in range(pps)]
    in_specs += [paged(cache_xt, idx) for idx in range(pps)]
    grid_spec = pltpu.PrefetchScalarGridSpec(
        num_scalar_prefetch=1,
        grid=(n, steps),
        in_specs=in_specs,
        out_specs=pl.BlockSpec((None, N_HEADS, HEAD_DIM, 1), lambda bi, g, pt: (bi, 0, 0, 0)),
        scratch_shapes=[
            pltpu.VMEM((N_HEADS, HEAD_DIM, page), F32), pltpu.VMEM((N_HEADS, page), F32),
            pltpu.VMEM((N_HEADS, page), F32), pltpu.VMEM((N_HEADS, HEAD_DIM, page), F32),
            pltpu.VMEM((N_HEADS, page), F32)],
    )
    args = [page_table, q, kn, vn, lfn, us] + [cache_kt] * pps + [cache_vt] * pps + [cache_xt] * pps
    return pl.pallas_call(
        functools.partial(_sample_attn_kernel, pps=pps),
        grid_spec=grid_spec,
        out_shape=jax.ShapeDtypeStruct((n, N_HEADS, HEAD_DIM, 1), F32),
        compiler_params=pltpu.CompilerParams(
            dimension_semantics=("arbitrary", "arbitrary"), vmem_limit_bytes=VMEM_LIMIT),
        name="sample_attn",
    )(*args)


def _suffix_matrix(page):
    src = np.arange(page)[:, None]
    dst = np.arange(page)[None, :]
    after = src > dst
    return jnp.asarray(np.concatenate([after, np.ones_like(after)], axis=1).astype(np.float32), BF16)


def _aux_constants():
    eq = np.zeros((3 * LANES, HEAD_TILES), np.float32)
    ek = np.zeros((3 * LANES, HEAD_TILES), np.float32)
    cq = np.zeros((1, HEAD_TILES), np.float32)
    ck = np.zeros((1, HEAD_TILES), np.float32)
    cv = np.zeros((1, HEAD_TILES), np.float32)
    for hd in range(N_HEADS):
        base = hd * LANES + _aux_base(hd)
        for part in range(3):
            eq[part * LANES + hd, base + part] = 1.0
            ek[part * LANES + hd, base + 3 + part] = -1.0
            cq[0, base + 3 + part] = 1.0
            ck[0, base + part] = 1.0
        cv[0, base] = 1.0
    return (jnp.asarray(eq, BF16), jnp.asarray(ek, BF16), jnp.asarray(cq), jnp.asarray(ck), jnp.asarray(cv))


def kernel(x_prompt, x_sample, cache_k, cache_v, cache_logf, state_conv, page_table, meta_tokens,
           w_in, b_forget, w_dw, b_dw, g_conv_ln, b_conv_ln, w_out, g_pre_mix, g_post_mix,
           g_pre_ffn, g_post_ffn, w_gate, w_up, w_down):
    depth = w_in.shape[0]
    b, seq, d = x_prompt.shape
    n = x_sample.shape[0]
    a = ATTN_WIDTH
    cw = w_dw.shape[2]
    page = cache_k.shape[2]
    t_real = N_META + seq
    tp = -(-t_real // ATTN_TILE) * ATTN_TILE

    scale = HEAD_DIM ** -0.5
    fg0 = 3 * a
    ca0 = fg0 + N_HEADS
    w_in_r = jnp.concatenate([
        w_in[:, :, 0:a] * scale, w_in[:, :, a:3 * a], w_in[:, :, ca0:ca0 + 2 * cw],
        jnp.pad(w_in[:, :, fg0:ca0], ((0, 0), (0, 0), (0, LANES - N_HEADS)))], axis=2).astype(BF16)
    vec = lambda p: p[:, None, :]
    bfg = vec(jnp.pad(b_forget, ((0, 0), (0, LANES - N_HEADS))))
    in_params = (vec(g_pre_mix), w_in_r, bfg)
    conv_params = (w_dw, vec(b_dw), vec(g_conv_ln), vec(b_conv_ln))
    out_params = (w_out[:, 0:a, :].astype(BF16), w_out[:, a:, :].astype(BF16), vec(g_post_mix),
                  vec(g_pre_ffn), vec(g_post_ffn), w_gate.astype(BF16), w_up.astype(BF16),
                  w_down.astype(BF16))
    tri = jnp.asarray(np.tril(np.ones((ROW_TILE, ROW_TILE), np.float32)), BF16)
    shared = (tri,) + _aux_constants()
    us = _suffix_matrix(page)
    cache_kt = jnp.transpose(cache_k, (0, 1, 3, 4, 2))
    cache_vt = jnp.transpose(cache_v, (0, 1, 3, 4, 2))
    cache_xt = jnp.transpose(cache_logf, (0, 1, 3, 2))
    state_t = jnp.transpose(state_conv, (0, 2, 1, 3))

    meta = jnp.broadcast_to(meta_tokens[None].astype(x_prompt.dtype), (b, N_META, d))
    xp = jnp.concatenate([meta, x_prompt, jnp.zeros((b, tp - t_real, d), x_prompt.dtype)], axis=1)
    xs = x_sample.reshape(n, d)

    result_bufs = ()
    cp, ks, vs, ls, cs = ([] for _ in range(5))
    for l in range(depth):
        qt, kt, vt, kbuf, vbuf, lbuf, conv, cst = _prompt_in(
            xp, l, depth, result_bufs, *in_params, *conv_params, shared, t_real)
        result_bufs = (kbuf, vbuf, lbuf)
        attn = _prompt_attn(qt, kt, vt)
        xp = _out_ffn(xp, attn, conv, l, *out_params, tm=OUT_TILE)
        cp.append(cst)

        qs, k_new, v_new, lf_new, conv_s, new_state = _sample_in(xs, l, *in_params, state_t, *conv_params)
        o = _sample_attn(page_table, qs[:, None, :], k_new[:, None, :], v_new[:, None, :],
                         lf_new[:, :, None], us, cache_kt, cache_vt, cache_xt, l)
        xs = _out_ffn(xs[None], o.reshape(1, n, a).astype(BF16), conv_s[None], l, *out_params, tm=n)[0]
        ks.append(k_new); vs.append(v_new); ls.append(lf_new); cs.append(new_state)

    kbuf, vbuf, lbuf = result_bufs
    hd = (N_HEADS, HEAD_DIM)
    per_head = lambda buf: jnp.transpose(buf.reshape(depth, b, *hd, t_real), (0, 1, 4, 2, 3))
    return (xp[:, N_META:t_real], xs[:, None, :],
            per_head(kbuf), per_head(vbuf), jnp.transpose(lbuf, (0, 1, 3, 2)), jnp.stack(cp),
            jnp.stack(ks).reshape(depth, n, 1, *hd), jnp.stack(vs).reshape(depth, n, 1, *hd),
            jnp.stack(ls)[:, :, None, :], jnp.transpose(jnp.stack(cs), (0, 2, 1, 3)))
```

```python
import functools

import numpy as np
import jax
import jax.numpy as jnp
from jax import lax
from jax.experimental import pallas as pl
from jax.experimental.pallas import tpu as pltpu

N_HEADS = 8
HEAD_DIM = 64
ATTN_WIDTH = N_HEADS * HEAD_DIM
N_META = 16
EPS = 1e-6
NEG_INF = -1e30

LANES = 128
SUBLANES = 8
HEAD_TILES = N_HEADS * LANES
ATTN_TILE = 768
DIAG_BLOCK = 256
ROW_TILE = 384
OUT_TILE = 768
CONV_CHUNK = 64
HIST_ROWS = 32
FF_CHUNK = 256
PAGES_PER_STEP = 16
VMEM_LIMIT = 56 * 1024 * 1024

F32 = jnp.float32
BF16 = jnp.bfloat16


def _aux_base(head):
    return HEAD_DIM if head % 2 == 0 else 0


def _resident(shape):
    zeros = (0,) * len(shape)
    return pl.BlockSpec(shape, lambda *_: zeros, pipeline_mode=pl.Buffered(1))


def _layer_block(arr, layer):
    idx = (layer,) + (0,) * (arr.ndim - 1)
    return pl.BlockSpec((None,) + arr.shape[1:], lambda *_: idx, pipeline_mode=pl.Buffered(1))


def _rmsnorm(x, g):
    return x * lax.rsqrt(jnp.mean(x * x, axis=-1, keepdims=True) + EPS) * g


def _log_sigmoid(x):
    return jnp.minimum(x, 0.0) - jnp.log1p(jnp.exp(-jnp.abs(x)))


def _sigmoid(x):
    return 1.0 / (1.0 + jnp.exp(-x))


def _split3(x):
    hi = x.astype(BF16)
    r1 = x - hi.astype(F32)
    mid = r1.astype(BF16)
    lo = (r1 - mid.astype(F32)).astype(BF16)
    return hi, mid, lo


def _conv_ln_silu(acc, g_ln, b_ln):
    mu = jnp.mean(acc, axis=-1, keepdims=True)
    xc = acc - mu
    y = xc * lax.rsqrt(jnp.mean(xc * xc, axis=-1, keepdims=True) + EPS) * g_ln + b_ln
    return y * _sigmoid(y)


def _prompt_in_kernel(kbuf_ref, vbuf_ref, lbuf_ref,
                      x_ref, g_ref, w_ref, bf_ref, wdw_ref, bdw_ref, gln_ref, bln_ref, tri_ref,
                      eq_ref, ek_ref, cq_ref, ck_ref, cv_ref,
                      q_ref, k_ref, v_ref, kt_ref, vt_ref, lt_ref, conv_ref, cst_ref,
                      ext_ref, sh_ref, carry_ref, *, tm, t_real, conv_k):
    del kbuf_ref, vbuf_ref, lbuf_ref
    i = pl.program_id(1)
    a = ATTN_WIDTH
    cw = conv_ref.shape[1]

    @pl.when(i == 0)
    def _():
        ext_ref[0:HIST_ROWS, :] = jnp.zeros((HIST_ROWS, cw), F32)
        carry_ref[...] = jnp.zeros_like(carry_ref)

    h = _rmsnorm(x_ref[...], g_ref[...]).astype(BF16)
    u = jnp.dot(h, w_ref[...], preferred_element_type=F32)
    uq = u[:, 0:a]
    uk = u[:, a:2 * a]
    uv = u[:, 2 * a:3 * a]
    kt_ref[...] = uk.T
    vt_ref[...] = uv.T

    fg0 = 3 * a + 2 * cw
    logf = _log_sigmoid(u[:, fg0:fg0 + LANES] + bf_ref[...])
    lt_ref[...] = logf.T[0:N_HEADS, :]
    tri = tri_ref[...]
    hi, mid, lo = _split3(logf)
    f = (jnp.dot(tri, hi, preferred_element_type=F32)
         + jnp.dot(tri, mid, preferred_element_type=F32)
         + jnp.dot(tri, lo, preferred_element_type=F32)) + carry_ref[...]
    carry_ref[...] = f[tm - 1:tm, :]
    fparts = jnp.concatenate(_split3(f), axis=1)
    aux_q = jnp.dot(fparts, eq_ref[...], preferred_element_type=F32) + cq_ref[...]
    aux_k = jnp.dot(fparts, ek_ref[...], preferred_element_type=F32) + ck_ref[...]
    lower = lax.broadcasted_iota(jnp.int32, (tm, LANES), 1) < HEAD_DIM
    for hd in range(N_HEADS):
        keep = lower if hd % 2 == 0 else jnp.logical_not(lower)
        pair = slice((hd // 2) * LANES, (hd // 2 + 1) * LANES)
        tile = slice(hd * LANES, (hd + 1) * LANES)
        q_ref[hd] = jnp.where(keep, uq[:, pair], aux_q[:, tile]).astype(BF16)
        k_ref[hd] = jnp.where(keep, uk[:, pair], aux_k[:, tile]).astype(BF16)
        v_ref[hd] = jnp.where(keep, uv[:, pair], cv_ref[:, tile]).T.astype(BF16)

    ca = u[:, 3 * a:3 * a + cw]
    cb = u[:, 3 * a + cw:3 * a + 2 * cw]
    ext_ref[HIST_ROWS:HIST_ROWS + tm, :] = ca * _sigmoid(cb)
    base = HIST_ROWS - (conv_k - 1)
    span = sh_ref.shape[1]
    for r in range(1, SUBLANES):
        sh_ref[r - 1] = ext_ref[r:r + span, :]
    bdw = bdw_ref[...]
    gln = gln_ref[...]
    bln = bln_ref[...]
    for c in range(tm // CONV_CHUNK):
        r0 = c * CONV_CHUNK
        acc = jnp.broadcast_to(bdw, (CONV_CHUNK, cw))
        for j in range(conv_k):
            r = (base + j) % SUBLANES
            a0 = r0 + base + j - r
            rows = ext_ref[a0:a0 + CONV_CHUNK, :] if r == 0 else sh_ref[r - 1, a0:a0 + CONV_CHUNK, :]
            acc = acc + wdw_ref[j:j + 1, :] * rows
        conv_ref[r0:r0 + CONV_CHUNK, :] = _conv_ln_silu(acc, gln, bln).astype(BF16)

    last = (t_real - 1) // tm

    @pl.when(i == last)
    def _():
        off = HIST_ROWS + (t_real - (conv_k - 1) - last * tm)
        cst_ref[...] = ext_ref[off:off + conv_k - 1, :]

    ext_ref[0:HIST_ROWS, :] = ext_ref[tm:tm + HIST_ROWS, :]


def _prompt_in(x, layer, result_bufs, g, w, bfg, wdw, bdw, gln, bln, shared, t_real):
    b, tp, d = x.shape
    depth = result_bufs[0].shape[0]
    tm = ROW_TILE
    a = ATTN_WIDTH
    cw = wdw.shape[2]
    conv_k = wdw.shape[1]
    assert conv_k - 1 <= HIST_ROWS and t_real - ((t_real - 1) // tm) * tm >= conv_k - 1
    row = lambda width: pl.BlockSpec((None, tm, width), lambda bi, i: (bi, i, 0))
    tiles = pl.BlockSpec((None, N_HEADS, tm, LANES), lambda bi, i: (bi, 0, i, 0))
    tiles_t = pl.BlockSpec((None, N_HEADS, LANES, tm), lambda bi, i: (bi, 0, 0, i))
    stacked_t = lambda rows: pl.BlockSpec((None, None, rows, tm), lambda bi, i: (layer, bi, 0, i))
    out_shape = (
        jax.ShapeDtypeStruct((b, N_HEADS, tp, LANES), BF16),
        jax.ShapeDtypeStruct((b, N_HEADS, tp, LANES), BF16),
        jax.ShapeDtypeStruct((b, N_HEADS, LANES, tp), BF16),
        jax.ShapeDtypeStruct((depth, b, a, t_real), F32),
        jax.ShapeDtypeStruct((depth, b, a, t_real), F32),
        jax.ShapeDtypeStruct((depth, b, N_HEADS, t_real), F32),
        jax.ShapeDtypeStruct((b, tp, cw), BF16),
        jax.ShapeDtypeStruct((b, conv_k - 1, cw), F32),
    )
    out_specs = (tiles, tiles, tiles_t, stacked_t(a), stacked_t(a), stacked_t(N_HEADS), row(cw),
                 pl.BlockSpec((None, conv_k - 1, cw), lambda bi, i: (bi, 0, 0)))
    stacked = (g, w, bfg, wdw, bdw, gln, bln)
    in_specs = ([pl.BlockSpec(memory_space=pl.ANY)] * len(result_bufs) + [row(d)]
                + [_layer_block(p, layer) for p in stacked] + [_resident(c.shape) for c in shared])
    return pl.pallas_call(
        functools.partial(_prompt_in_kernel, tm=tm, t_real=t_real, conv_k=conv_k),
        grid=(b, tp // tm),
        in_specs=in_specs,
        out_specs=out_specs,
        out_shape=out_shape,
        input_output_aliases={0: 3, 1: 4, 2: 5},
        scratch_shapes=[pltpu.VMEM((tm + HIST_ROWS, cw), F32),
                        pltpu.VMEM((SUBLANES - 1, tm + HIST_ROWS - SUBLANES, cw), F32),
                        pltpu.VMEM((1, LANES), F32)],
        compiler_params=pltpu.CompilerParams(
            dimension_semantics=("arbitrary", "arbitrary"), vmem_limit_bytes=VMEM_LIMIT),
        name="prompt_in",
    )(*result_bufs, x, *stacked, *shared)


def _prompt_attn_kernel(q_ref, k_ref, v_ref, o_ref, sa_ref, sb_ref, m_ref, acc_ref, *, tq):
    i = pl.program_id(2)
    m_ref[...] = jnp.full(m_ref.shape, NEG_INF, F32)
    acc_ref[...] = jnp.zeros_like(acc_ref)

    def scores(j, s_ref):
        start = pl.multiple_of(j * tq, tq)
        for hh in range(2):
            s_ref[hh] = lax.dot_general(k_ref[hh, pl.ds(start, tq), :], q_ref[hh],
                                        (((1,), (1,)), ((), ())), preferred_element_type=F32)

    def consume(j, s_ref):
        start = pl.multiple_of(j * tq, tq)
        for hh in range(2):
            s = s_ref[hh]
            m_prev = m_ref[hh]
            m_new = jnp.maximum(m_prev, jnp.max(s, axis=0, keepdims=True))
            alpha = jnp.exp(m_prev - m_new)
            p = jnp.exp(s - m_new)
            acc_ref[hh] = alpha * acc_ref[hh] + jnp.dot(
                v_ref[hh, :, pl.ds(start, tq)], p.astype(BF16), preferred_element_type=F32)
            m_ref[hh] = m_new

    def consume_diag(s_ref):
        base = pl.multiple_of(i * tq, tq)
        for hh in range(2):
            for q0 in range(0, tq, DIAG_BLOCK):
                nk = q0 + DIAG_BLOCK
                s = s_ref[hh, 0:nk, q0:nk]
                key = lax.broadcasted_iota(jnp.int32, (nk, DIAG_BLOCK), 0)
                qry = lax.broadcasted_iota(jnp.int32, (nk, DIAG_BLOCK), 1) + q0
                s = jnp.where(key <= qry, s, NEG_INF)
                m_prev = m_ref[hh, :, q0:nk]
                m_new = jnp.maximum(m_prev, jnp.max(s, axis=0, keepdims=True))
                alpha = jnp.exp(m_prev - m_new)
                p = jnp.exp(s - m_new)
                acc_ref[hh, :, q0:nk] = alpha * acc_ref[hh, :, q0:nk] + jnp.dot(
                    v_ref[hh, :, pl.ds(base, nk)], p.astype(BF16), preferred_element_type=F32)
                m_ref[hh, :, q0:nk] = m_new

    scores(0, sa_ref)

    def body(t, carry):
        scores(2 * t + 1, sb_ref)
        consume(2 * t, sa_ref)
        scores(2 * t + 2, sa_ref)
        consume(2 * t + 1, sb_ref)
        return carry

    lax.fori_loop(0, i // 2, body, 0)

    @pl.when(i % 2 == 0)
    def _():
        consume_diag(sa_ref)

    @pl.when(i % 2 == 1)
    def _():
        scores(i, sb_ref)
        consume(i - 1, sa_ref)
        consume_diag(sb_ref)

    outs = []
    for hh in range(2):
        acc = acc_ref[hh]
        ab = _aux_base(hh)
        outs.append((acc / acc[ab:ab + 1, :]).T)
    lane = lax.broadcasted_iota(jnp.int32, (tq, LANES), 1)
    o_ref[...] = jnp.where(lane < HEAD_DIM, outs[0], outs[1]).astype(o_ref.dtype)


def _prompt_attn(q, k, vt):
    b, nh, tp, _ = q.shape
    tq = ATTN_TILE
    qspec = pl.BlockSpec((None, 2, tq, LANES), lambda bi, hp, i: (bi, hp, i, 0))
    kspec = pl.BlockSpec((None, 2, tp, LANES), lambda bi, hp, i: (bi, hp, 0, 0))
    vspec = pl.BlockSpec((None, 2, LANES, tp), lambda bi, hp, i: (bi, hp, 0, 0))
    return pl.pallas_call(
        functools.partial(_prompt_attn_kernel, tq=tq),
        grid=(b, nh // 2, tp // tq),
        in_specs=[qspec, kspec, vspec],
        out_specs=pl.BlockSpec((None, tq, LANES), lambda bi, hp, i: (bi, i, hp)),
        out_shape=jax.ShapeDtypeStruct((b, tp, ATTN_WIDTH), BF16),
        scratch_shapes=[pltpu.VMEM((2, tq, tq), F32), pltpu.VMEM((2, tq, tq), F32),
                        pltpu.VMEM((2, 1, tq), F32), pltpu.VMEM((2, LANES, tq), F32)],
        compiler_params=pltpu.CompilerParams(
            dimension_semantics=("arbitrary", "arbitrary", "arbitrary"), vmem_limit_bytes=VMEM_LIMIT),
        name="prompt_attn",
    )(q, k, vt)


def _out_ffn_kernel(x_ref, attn_ref, conv_ref, woa_ref, woc_ref, gpm_ref, gpf_ref, gqf_ref,
                    wg_ref, wu_ref, wd_ref, o_ref, *, d_ff):
    mixed = (jnp.dot(attn_ref[...], woa_ref[...], preferred_element_type=F32)
             + jnp.dot(conv_ref[...], woc_ref[...], preferred_element_type=F32))
    x1 = x_ref[...] + _rmsnorm(mixed, gpm_ref[...])
    h = _rmsnorm(x1, gpf_ref[...]).astype(BF16)
    ff = jnp.zeros(x1.shape, F32)
    for c in range(d_ff // FF_CHUNK):
        c0 = c * FF_CHUNK
        gate = jnp.dot(h, wg_ref[:, c0:c0 + FF_CHUNK], preferred_element_type=F32)
        up = jnp.dot(h, wu_ref[:, c0:c0 + FF_CHUNK], preferred_element_type=F32)
        act = (gate * _sigmoid(gate) * up).astype(BF16)
        ff = ff + jnp.dot(act, wd_ref[c0:c0 + FF_CHUNK, :], preferred_element_type=F32)
    o_ref[...] = x1 + _rmsnorm(ff, gqf_ref[...])


def _out_ffn(x, attn, conv, layer, woa, woc, gpm, gpf, gqf, wg, wu, wd, tm):
    b, tp, d = x.shape
    d_ff = wg.shape[2]
    row = lambda width: pl.BlockSpec((None, tm, width), lambda bi, i: (bi, i, 0))
    stacked = (woa, woc, gpm, gpf, gqf, wg, wu, wd)
    return pl.pallas_call(
        functools.partial(_out_ffn_kernel, d_ff=d_ff),
        grid=(b, tp // tm),
        in_specs=[row(d), row(attn.shape[2]), row(conv.shape[2])] + [_layer_block(p, layer) for p in stacked],
        out_specs=row(d),
        out_shape=jax.ShapeDtypeStruct((b, tp, d), F32),
        compiler_params=pltpu.CompilerParams(
            dimension_semantics=("arbitrary", "arbitrary"), vmem_limit_bytes=VMEM_LIMIT),
        name="out_ffn",
    )(x, attn, conv, *stacked)


def _sample_in_kernel(x_ref, g_ref, w_ref, bf_ref, st_ref, wdw_ref, bdw_ref, gln_ref, bln_ref,
                      q_ref, k_ref, v_ref, lf_ref, conv_ref, ns_ref, *, conv_k):
    a = ATTN_WIDTH
    cw = conv_ref.shape[1]
    h = _rmsnorm(x_ref[...], g_ref[...]).astype(BF16)
    u = jnp.dot(h, w_ref[...], preferred_element_type=F32)
    q_ref[...] = u[:, 0:a]
    k_ref[...] = u[:, a:2 * a]
    v_ref[...] = u[:, 2 * a:3 * a]
    fg0 = 3 * a + 2 * cw
    lf_ref[...] = _log_sigmoid(u[:, fg0:fg0 + LANES] + bf_ref[...])[:, 0:N_HEADS]
    glu = u[:, 3 * a:3 * a + cw] * _sigmoid(u[:, 3 * a + cw:3 * a + 2 * cw])
    acc = bdw_ref[...] + wdw_ref[conv_k - 1:conv_k, :] * glu
    for j in range(conv_k - 1):
        acc = acc + wdw_ref[j:j + 1, :] * st_ref[j]
    conv_ref[...] = _conv_ln_silu(acc, gln_ref[...], bln_ref[...]).astype(BF16)
    for j in range(conv_k - 2):
        ns_ref[j] = st_ref[j + 1]
    ns_ref[conv_k - 2] = glu


def _sample_in(x, layer, g, w, bfg, state_t, wdw, bdw, gln, bln):
    n, d = x.shape
    a = ATTN_WIDTH
    cw = wdw.shape[2]
    conv_k = wdw.shape[1]
    out = lambda shape, dt: (jax.ShapeDtypeStruct(shape, dt), pl.BlockSpec(shape, lambda i: (0,) * len(shape)))
    outs = [out((n, a), F32), out((n, a), F32), out((n, a), F32), out((n, N_HEADS), F32),
            out((n, cw), BF16), out(state_t.shape[1:], F32)]
    stacked = (g, w, bfg, state_t, wdw, bdw, gln, bln)
    return pl.pallas_call(
        functools.partial(_sample_in_kernel, conv_k=conv_k),
        grid=(1,),
        in_specs=[pl.BlockSpec(x.shape, lambda i: (0, 0))] + [_layer_block(p, layer) for p in stacked],
        out_specs=tuple(o[1] for o in outs),
        out_shape=tuple(o[0] for o in outs),
        compiler_params=pltpu.CompilerParams(
            dimension_semantics=("arbitrary",), vmem_limit_bytes=VMEM_LIMIT),
        name="sample_in",
    )(x, *stacked)


def _sample_attn_kernel(pt_ref, q_ref, kn_ref, vn_ref, lfn_ref, us_ref, *rest, pps):
    k_refs = rest[0:pps]
    v_refs = rest[pps:2 * pps]
    x_refs = rest[2 * pps:3 * pps]
    o_ref = rest[3 * pps]
    qb_ref, m_ref, l_ref, acc_ref, carry_ref = rest[3 * pps + 1:]
    g = pl.program_id(1)
    page = qb_ref.shape[2]

    eye = (lax.broadcasted_iota(jnp.int32, (HEAD_DIM, HEAD_DIM), 0)
           == lax.broadcasted_iota(jnp.int32, (HEAD_DIM, HEAD_DIM), 1))

    def head_column(row_ref, hd):
        row = row_ref[:, hd * HEAD_DIM:(hd + 1) * HEAD_DIM]
        return jnp.sum(jnp.where(eye, row, 0.0), axis=1, keepdims=True)

    @pl.when(g == 0)
    def _():
        for hd in range(N_HEADS):
            qb_ref[hd] = jnp.broadcast_to(head_column(q_ref, hd), (HEAD_DIM, page))
        m_ref[...] = jnp.full(m_ref.shape, NEG_INF, F32)
        l_ref[...] = jnp.zeros_like(l_ref)
        acc_ref[...] = jnp.zeros_like(acc_ref)
        carry_ref[...] = jnp.zeros_like(carry_ref)

    us = us_ref[...]
    logf = jnp.concatenate([x_refs[idx][...] for idx in range(pps)], axis=0)
    hi, mid, lo = _split3(logf)
    yr = (jnp.dot(hi, us, preferred_element_type=F32)
          + jnp.dot(mid, us, preferred_element_type=F32)
          + jnp.dot(lo, us, preferred_element_type=F32))
    carry = carry_ref[...]
    lfn = lfn_ref[...]
    scores = []
    for idx in range(pps):
        rows = slice(idx * N_HEADS, (idx + 1) * N_HEADS)
        qk = jnp.concatenate(
            [jnp.sum(qb_ref[hd] * k_refs[idx][hd], axis=0, keepdims=True) for hd in range(N_HEADS)], axis=0)
        scores.append(qk + (yr[rows, 0:page] + carry + lfn))
        carry = carry + yr[rows, page:2 * page]
    carry_ref[...] = carry

    m_prev = m_ref[...]
    m_new = m_prev
    for s in scores:
        m_new = jnp.maximum(m_new, s)
    alpha = jnp.exp(m_prev - m_new)
    probs = [jnp.exp(s - m_new) for s in scores]
    l_new = alpha * l_ref[...]
    for p in probs:
        l_new = l_new + p
    l_ref[...] = l_new
    m_ref[...] = m_new
    for hd in range(N_HEADS):
        acc = alpha[hd:hd + 1, :] * acc_ref[hd]
        for idx in range(pps):
            acc = acc + probs[idx][hd:hd + 1, :] * v_refs[idx][hd]
        acc_ref[hd] = acc

    @pl.when(g == pl.num_programs(1) - 1)
    def _():
        for hd in range(N_HEADS):
            m = m_ref[hd:hd + 1, :]
            m_all = jnp.max(m, axis=1, keepdims=True)
            w = jnp.exp(m - m_all)
            l_all = jnp.sum(l_ref[hd:hd + 1, :] * w, axis=1, keepdims=True)
            o_all = jnp.sum(acc_ref[hd] * w, axis=1, keepdims=True)
            cols = slice(hd * HEAD_DIM, (hd + 1) * HEAD_DIM)
            s_new = jnp.sum(q_ref[:, cols] * kn_ref[:, cols], axis=1, keepdims=True)
            m_fin = jnp.maximum(m_all, s_new)
            a_old = jnp.exp(m_all - m_fin)
            p_new = jnp.exp(s_new - m_fin)
            o_ref[hd] = (a_old * o_all + p_new * head_column(vn_ref, hd)) / (a_old * l_all + p_new)


def _sample_attn(page_table, q, kn, vn, lfn, us, cache_kt, cache_vt, cache_xt, layer):
    n, n_pages = page_table.shape
    page = cache_kt.shape[4]
    pps = PAGES_PER_STEP
    steps = n_pages // pps
    per_seq = lambda arr: pl.BlockSpec((None,) + arr.shape[1:], lambda bi, g, pt: (bi,) + (0,) * (arr.ndim - 1))

    def paged(arr, idx):
        zeros = (0,) * (arr.ndim - 2)
        return pl.BlockSpec(
            (None, None) + arr.shape[2:],
            lambda bi, g, pt: (layer, pt[bi, n_pages - 1 - (g * pps + idx)]) + zeros)

    in_specs = [per_seq(q), per_seq(kn), per_seq(vn), per_seq(lfn),
                pl.BlockSpec(us.shape, lambda bi, g, pt: (0, 0))]
    in_specs += [paged(cache_kt, idx) for idx in range(pps)]
    in_specs += [paged(cache_vt, idx) for idx in range(pps)]
    in_specs += [paged(cache_xt, idx) for idx in range(pps)]
    grid_spec = pltpu.PrefetchScalarGridSpec(
        num_scalar_prefetch=1,
        grid=(n, steps),
        in_specs=in_specs,
        out_specs=pl.BlockSpec((None, N_HEADS, HEAD_DIM, 1), lambda bi, g, pt: (bi, 0, 0, 0)),
        scratch_shapes=[
            pltpu.VMEM((N_HEADS, HEAD_DIM, page), F32), pltpu.VMEM((N_HEADS, page), F32),
            pltpu.VMEM((N_HEADS, page), F32), pltpu.VMEM((N_HEADS, HEAD_DIM, page), F32),
            pltpu.VMEM((N_HEADS, page), F32)],
    )
    args = [page_table, q, kn, vn, lfn, us] + [cache_kt] * pps + [cache_vt] * pps + [cache_xt] * pps
    return pl.pallas_call(
        functools.partial(_sample_attn_kernel, pps=pps),
        grid_spec=grid_spec,
        out_shape=jax.ShapeDtypeStruct((n, N_HEADS, HEAD_DIM, 1), F32),
        compiler_params=pltpu.CompilerParams(
            dimension_semantics=("arbitrary", "arbitrary"), vmem_limit_bytes=VMEM_LIMIT),
        name="sample_attn",
    )(*args)


def _suffix_matrix(page):
    src = np.arange(page)[:, None]
    dst = np.arange(page)[None, :]
    after = src > dst
    return jnp.asarray(np.concatenate([after, np.ones_like(after)], axis=1).astype(np.float32), BF16)


def _aux_constants():
    eq = np.zeros((3 * LANES, HEAD_TILES), np.float32)
    ek = np.zeros((3 * LANES, HEAD_TILES), np.float32)
    cq = np.zeros((1, HEAD_TILES), np.float32)
    ck = np.zeros((1, HEAD_TILES), np.float32)
    cv = np.zeros((1, HEAD_TILES), np.float32)
    for hd in range(N_HEADS):
        base = hd * LANES + _aux_base(hd)
        for part in range(3):
            eq[part * LANES + hd, base + part] = 1.0
            ek[part * LANES + hd, base + 3 + part] = -1.0
            cq[0, base + 3 + part] = 1.0
            ck[0, base + part] = 1.0
        cv[0, base] = 1.0
    return (jnp.asarray(eq, BF16), jnp.asarray(ek, BF16), jnp.asarray(cq), jnp.asarray(ck), jnp.asarray(cv))


def kernel(x_prompt, x_sample, cache_k, cache_v, cache_logf, state_conv, page_table, meta_tokens,
           w_in, b_forget, w_dw, b_dw, g_conv_ln, b_conv_ln, w_out, g_pre_mix, g_post_mix,
           g_pre_ffn, g_post_ffn, w_gate, w_up, w_down):
    depth = w_in.shape[0]
    b, seq, d = x_prompt.shape
    n = x_sample.shape[0]
    a = ATTN_WIDTH
    cw = w_dw.shape[2]
    page = cache_k.shape[2]
    t_real = N_META + seq
    tp = -(-t_real // ATTN_TILE) * ATTN_TILE

    scale = HEAD_DIM ** -0.5
    fg0 = 3 * a
    ca0 = fg0 + N_HEADS
    w_in_r = jnp.concatenate([
        w_in[:, :, 0:a] * scale, w_in[:, :, a:3 * a], w_in[:, :, ca0:ca0 + 2 * cw],
        jnp.pad(w_in[:, :, fg0:ca0], ((0, 0), (0, 0), (0, LANES - N_HEADS)))], axis=2).astype(BF16)
    vec = lambda p: p[:, None, :]
    bfg = vec(jnp.pad(b_forget, ((0, 0), (0, LANES - N_HEADS))))
    in_params = (vec(g_pre_mix), w_in_r, bfg)
    conv_params = (w_dw, vec(b_dw), vec(g_conv_ln), vec(b_conv_ln))
    out_params = (w_out[:, 0:a, :].astype(BF16), w_out[:, a:, :].astype(BF16), vec(g_post_mix),
                  vec(g_pre_ffn), vec(g_post_ffn), w_gate.astype(BF16), w_up.astype(BF16),
                  w_down.astype(BF16))
    tri = jnp.asarray(np.tril(np.ones((ROW_TILE, ROW_TILE), np.float32)), BF16)
    shared = (tri,) + _aux_constants()
    us = _suffix_matrix(page)
    cache_kt = jnp.transpose(cache_k, (0, 1, 3, 4, 2))
    cache_vt = jnp.transpose(cache_v, (0, 1, 3, 4, 2))
    cache_xt = jnp.transpose(cache_logf, (0, 1, 3, 2))
    state_t = jnp.transpose(state_conv, (0, 2, 1, 3))

    meta = jnp.broadcast_to(meta_tokens[None].astype(x_prompt.dtype), (b, N_META, d))
    xp = jnp.concatenate([meta, x_prompt, jnp.zeros((b, tp - t_real, d), x_prompt.dtype)], axis=1)
    xs = x_sample.reshape(n, d)

    result_bufs = (jnp.zeros((depth, b, a, t_real), F32), jnp.zeros((depth, b, a, t_real), F32),
                   jnp.zeros((depth, b, N_HEADS, t_real), F32))
    cp, ks, vs, ls, cs = ([] for _ in range(5))
    for l in range(depth):
        qt, kt, vt, kbuf, vbuf, lbuf, conv, cst = _prompt_in(
            xp, l, result_bufs, *in_params, *conv_params, shared, t_real)
        result_bufs = (kbuf, vbuf, lbuf)
        attn = _prompt_attn(qt, kt, vt)
        xp = _out_ffn(xp, attn, conv, l, *out_params, tm=OUT_TILE)
        cp.append(cst)

        qs, k_new, v_new, lf_new, conv_s, new_state = _sample_in(xs, l, *in_params, state_t, *conv_params)
        o = _sample_attn(page_table, qs[:, None, :], k_new[:, None, :], v_new[:, None, :],
                         lf_new[:, :, None], us, cache_kt, cache_vt, cache_xt, l)
        xs = _out_ffn(xs[None], o.reshape(1, n, a).astype(BF16), conv_s[None], l, *out_params, tm=n)[0]
        ks.append(k_new); vs.append(v_new); ls.append(lf_new); cs.append(new_state)

    kbuf, vbuf, lbuf = result_bufs
    hd = (N_HEADS, HEAD_DIM)
    per_head = lambda buf: jnp.transpose(buf.reshape(depth, b, *hd, t_real), (0, 1, 4, 2, 3))
    return (xp[:, N_META:t_real], xs[:, None, :],
            per_head(kbuf), per_head(vbuf), jnp.transpose(lbuf, (0, 1, 3, 2)), jnp.stack(cp),
            jnp.stack(ks).reshape(depth, n, 1, *hd), jnp.stack(vs).reshape(depth, n, 1, *hd),
            jnp.stack(ls)[:, :, None, :], jnp.transpose(jnp.stack(cs), (0, 2, 1, 3)))
```
